```python
import math
import jax, jax.numpy as jnp
from jax import lax
import numpy as np

D_MODEL = 1024
BATCH = 4
SEQ = 8192
DEPTH = 1

PLE_DIM = 256
CHUNK = 128
A_GROUPS = 8
A_GROUP_DIM = D_MODEL // A_GROUPS
A_WIDTH = A_GROUPS * A_GROUP_DIM
B_HEADS = 8
B_QK_DIM = 64
B_V_DIM = 2 * B_QK_DIM
B_QK_WIDTH = B_HEADS * 2 * B_QK_DIM
B_WIDTH = B_HEADS * B_V_DIM
Q_BLOCK = 128
ROPE_THETA = 10000.0
LN_EPS = 1e-5
RMS_EPS = 1e-5
DEEPNORM_ALPHA = (2 * DEPTH) ** 0.25
DEEPNORM_BETA = (8 * DEPTH) ** -0.25
IN_SIZES = (A_WIDTH, A_WIDTH, A_WIDTH, B_QK_WIDTH, B_QK_WIDTH, B_WIDTH, B_WIDTH, D_MODEL, D_MODEL)
IN_WIDTH = sum(IN_SIZES)
IN_OFFSETS = tuple(int(o) for o in np.cumsum(IN_SIZES)[:-1])
B_V_OFFSET = 3 * A_WIDTH + 2 * B_QK_WIDTH

kernel_name = "hybrid_gmlp_diffattn_gated_deepnorm"


def layer_norm(x, g, b):
    xf = x.astype(jnp.float32)
    mu = jnp.mean(xf, axis=-1, keepdims=True)
    xc = xf - mu
    var = jnp.mean(xc * xc, axis=-1, keepdims=True)
    y = xc * lax.rsqrt(var + LN_EPS) * g.astype(jnp.float32) + b.astype(jnp.float32)
    return y.astype(x.dtype)


def rms_norm(x, g):
    xf = x.astype(jnp.float32)
    y = xf * lax.rsqrt(jnp.mean(xf * xf, axis=-1, keepdims=True) + RMS_EPS) * g.astype(jnp.float32)
    return y.astype(x.dtype)


def rope_tables(positions, dim):
    inv_freq = ROPE_THETA ** (-jnp.arange(0, dim, 2, dtype=jnp.float32) / dim)
    ang = positions.astype(jnp.float32)[..., None] * inv_freq
    return jnp.cos(ang), jnp.sin(ang)


def apply_rope(t, cos, sin):
    c = cos[:, :, None, None, :].astype(t.dtype)
    s = sin[:, :, None, None, :].astype(t.dtype)
    t1, t2 = jnp.split(t, 2, axis=-1)
    return jnp.concatenate([t1 * c - t2 * s, t2 * c + t1 * s], axis=-1)


def diff_attention(q, k, v, lam):
    bn, s_len = q.shape[0], q.shape[1]
    nb = s_len // Q_BLOCK
    qb = q.reshape(bn, nb, Q_BLOCK, B_HEADS, 2, B_QK_DIM).transpose(1, 0, 2, 3, 4, 5)
    starts = jnp.arange(nb, dtype=jnp.int32) * Q_BLOCK
    kpos = jnp.arange(s_len, dtype=jnp.int32)

    def block(args):
        qblk, start = args
        qpos = start + jnp.arange(Q_BLOCK, dtype=jnp.int32)
        mask = kpos[None, :] <= qpos[:, None]
        sc = jnp.einsum("bqhmd,bkhmd->bhmqk", qblk, k,
                        preferred_element_type=jnp.float32)
        sc = jnp.where(mask, sc, -jnp.inf)
        pr = jax.nn.softmax(sc, axis=-1)
        a = pr[:, :, 0] - lam * pr[:, :, 1]
        return jnp.einsum("bhqk,bkhd->bqhd", a.astype(v.dtype), v)

    o = lax.map(block, (qb, starts))
    return o.transpose(1, 0, 2, 3, 4).reshape(bn, s_len, B_HEADS, B_V_DIM)


def setup_inputs(seed: int = 0) -> dict:
    key = jax.random.key(seed)
    ks = jax.random.split(key, 24)
    f32 = jnp.float32
    nrm = lambda k, shape, scale: jax.random.normal(k, shape, f32) * scale
    x = nrm(ks[0], (BATCH, SEQ, D_MODEL), 1.0)
    p = nrm(ks[1], (DEPTH, BATCH, SEQ, PLE_DIM), 1.0)
    offset = jax.random.randint(ks[2], (BATCH, 1), 0, 1024, dtype=jnp.int32)
    positions = offset + jnp.arange(SEQ, dtype=jnp.int32)[None, :]
    w_in = nrm(ks[3], (DEPTH, D_MODEL, IN_WIDTH), D_MODEL ** -0.5)
    w_in = w_in.at[:, :, B_V_OFFSET:B_V_OFFSET + B_WIDTH].multiply(DEEPNORM_BETA)
    a_ln_g = 1.0 + nrm(ks[4], (DEPTH, A_WIDTH), 0.05)
    a_ln_b = nrm(ks[5], (DEPTH, A_WIDTH), 0.02)
    a_w_s = nrm(ks[6], (DEPTH, A_GROUPS, CHUNK, CHUNK), CHUNK ** -0.5)
    a_b_s = 1.0 + nrm(ks[7], (DEPTH, A_GROUPS, CHUNK), 0.1)
    b_lam_q1 = nrm(ks[8], (DEPTH, B_QK_DIM), 0.1)
    b_lam_k1 = nrm(ks[9], (DEPTH, B_QK_DIM), 0.1)
    b_lam_q2 = nrm(ks[10], (DEPTH, B_QK_DIM), 0.1)
    b_lam_k2 = nrm(ks[11], (DEPTH, B_QK_DIM), 0.1)
    b_subln_g = 1.0 + nrm(ks[12], (DEPTH, B_V_DIM), 0.05)
    w_branch_a = nrm(ks[13], (DEPTH, A_WIDTH, D_MODEL), A_WIDTH ** -0.5 * DEEPNORM_BETA)
    w_branch_b = nrm(ks[14], (DEPTH, B_WIDTH, D_MODEL), B_WIDTH ** -0.5 * DEEPNORM_BETA)
    w_out = nrm(ks[15], (DEPTH, D_MODEL, D_MODEL), D_MODEL ** -0.5 * DEEPNORM_BETA)
    w_ple = nrm(ks[16], (DEPTH, PLE_DIM, D_MODEL), PLE_DIM ** -0.5 * DEEPNORM_BETA)
    w_ple_gate = nrm(ks[17], (DEPTH, D_MODEL, D_MODEL), D_MODEL ** -0.5)
    ln_g = 1.0 + nrm(ks[18], (DEPTH, D_MODEL), 0.05)
    ln_b = nrm(ks[19], (DEPTH, D_MODEL), 0.02)
    return {"x": x, "p": p, "positions": positions, "w_in": w_in,
            "a_ln_g": a_ln_g, "a_ln_b": a_ln_b, "a_w_s": a_w_s, "a_b_s": a_b_s,
            "b_lam_q1": b_lam_q1, "b_lam_k1": b_lam_k1, "b_lam_q2": b_lam_q2, "b_lam_k2": b_lam_k2,
            "b_subln_g": b_subln_g, "w_branch_a": w_branch_a, "w_branch_b": w_branch_b,
            "w_out": w_out, "w_ple": w_ple, "w_ple_gate": w_ple_gate,
            "ln_g": ln_g, "ln_b": ln_b}


def reference(x, p, positions, w_in, a_ln_g, a_ln_b, a_w_s, a_b_s,
              b_lam_q1, b_lam_k1, b_lam_q2, b_lam_k2, b_subln_g,
              w_branch_a, w_branch_b, w_out, w_ple, w_ple_gate, ln_g, ln_b):
    bn, s_len, _ = x.shape
    n_chunks = s_len // CHUNK
    cos, sin = rope_tables(positions, B_QK_DIM)
    causal_chunk = jnp.tril(jnp.ones((CHUNK, CHUNK), dtype=bool))
    for i in range(DEPTH):
        lam_init = 0.8 - 0.6 * math.exp(-0.3 * i)
        h = x @ w_in[i]
        ua, va, za, qh, kh, vh, zb, ga, gb = jnp.split(h, IN_OFFSETS, axis=-1)

        ua = jax.nn.gelu(ua, approximate=False)
        va = layer_norm(jax.nn.gelu(va, approximate=False), a_ln_g[i], a_ln_b[i])
        vc = va.reshape(bn, n_chunks, CHUNK, A_GROUPS, A_GROUP_DIM)
        ws = jnp.where(causal_chunk, a_w_s[i], 0.0)
        sa = jnp.einsum("gts,bnsgc->bntgc", ws, vc) + a_b_s[i].T[:, :, None]
        ya = ua * sa.reshape(bn, s_len, A_WIDTH) * jax.nn.silu(za)

        q = apply_rope(qh.reshape(bn, s_len, B_HEADS, 2, B_QK_DIM), cos, sin) * (B_QK_DIM ** -0.5)
        k = apply_rope(kh.reshape(bn, s_len, B_HEADS, 2, B_QK_DIM), cos, sin)
        lam = (jnp.exp(jnp.sum(b_lam_q1[i].astype(jnp.float32) * b_lam_k1[i].astype(jnp.float32)))
               - jnp.exp(jnp.sum(b_lam_q2[i].astype(jnp.float32) * b_lam_k2[i].astype(jnp.float32)))
               + lam_init)
        o = diff_attention(q, k, vh.reshape(bn, s_len, B_HEADS, B_V_DIM), lam)
        o = rms_norm(o, b_subln_g[i]) * (1.0 - lam_init)
        yb = o.reshape(bn, s_len, B_WIDTH) * jax.nn.silu(zb)

        merged = jax.nn.sigmoid(ga) * (ya @ w_branch_a[i]) + jax.nn.sigmoid(gb) * (yb @ w_branch_b[i])
        mix_out = merged @ w_out[i]

        ple = jax.nn.sigmoid(x @ w_ple_gate[i]) * (p[i] @ w_ple[i])

        x = layer_norm(DEEPNORM_ALPHA * x + mix_out + ple, ln_g[i], ln_b[i])
    return x
```

```python
import functools
import math

import jax
import jax.numpy as jnp
from jax import lax
from jax.experimental import pallas as pl
from jax.experimental.pallas import tpu as pltpu

F32 = jnp.float32
BF16 = jnp.bfloat16

D_MODEL = 1024
PLE_DIM = 256
CHUNK = 128
A_GROUPS = 8
HEADS = 8
QK_DIM = 64
HEAD_DIM = 2 * QK_DIM
ROPE_THETA = 10000.0
LN_EPS = 1e-5
RMS_EPS = 1e-5
NEG_BIG = -1e30

TOKEN_TILE = 512
COL_TILE = 256
ATTN_TILE = 512
VMEM_LIMIT_BYTES = 48 * 1024 * 1024


def _dot(a, b):
    return jnp.dot(a, b, preferred_element_type=F32)


def _gelu(x):
    return 0.5 * x * (1.0 + lax.erf(x * math.sqrt(0.5)))


def _sigmoid(x):
    return jax.nn.sigmoid(x)


def _layer_norm(x, g, b):
    mu = jnp.mean(x, axis=-1, keepdims=True)
    xc = x - mu
    var = jnp.mean(xc * xc, axis=-1, keepdims=True)
    return xc * lax.rsqrt(var + LN_EPS) * g + b


def _mixer_a_kernel(x_ref, wu_ref, wv_ref, wz_ref, wg_ref, lng_ref, lnb_ref, ws_ref, bst_ref,
                    wba_ref, out_ref, vb_ref, ya_ref):
    tm = x_ref.shape[0]
    xb = x_ref[...].astype(BF16)
    va = _gelu(_dot(xb, wv_ref[...]))
    vb_ref[...] = _layer_norm(va, lng_ref[...], lnb_ref[...]).astype(BF16)

    row = lax.broadcasted_iota(jnp.int32, (CHUNK, CHUNK), 0)
    col = lax.broadcasted_iota(jnp.int32, (CHUNK, CHUNK), 1)
    causal = col <= row
    groups_per_block = COL_TILE // CHUNK
    for cb in range(D_MODEL // COL_TILE):
        cols = slice(cb * COL_TILE, (cb + 1) * COL_TILE)
        u = _gelu(_dot(xb, wu_ref[:, cols]))
        z = _dot(xb, wz_ref[:, cols])
        sz = z * _sigmoid(z)
        for gi in range(groups_per_block):
            g = cb * groups_per_block + gi
            gcols = slice(g * CHUNK, (g + 1) * CHUNK)
            lcols = slice(gi * CHUNK, (gi + 1) * CHUNK)
            ws_g = jnp.where(causal, ws_ref[g], 0.0).astype(BF16)
            bias = bst_ref[:, g:g + 1]
            for n in range(tm // CHUNK):
                rows = slice(n * CHUNK, (n + 1) * CHUNK)
                sa = _dot(ws_g, vb_ref[rows, gcols]) + bias
                ya_ref[rows, gcols] = (u[rows, lcols] * sa * sz[rows, lcols]).astype(BF16)

    pa = _dot(ya_ref[...], wba_ref[...])
    ga = _dot(xb, wg_ref[...])
    out_ref[...] = (_sigmoid(ga) * pa).astype(out_ref.dtype)


def _mixer_a(x2, wu, wv, wz, wg, lng, lnb, ws, bst, wba):
    n_tok = x2.shape[0]
    tm = TOKEN_TILE
    full = lambda shape: pl.BlockSpec(shape, lambda i: (0,) * len(shape))
    return pl.pallas_call(
        _mixer_a_kernel,
        grid=(n_tok // tm,),
        in_specs=[
            pl.BlockSpec((tm, D_MODEL), lambda i: (i, 0)),
            full((D_MODEL, D_MODEL)), full((D_MODEL, D_MODEL)), full((D_MODEL, D_MODEL)),
            full((D_MODEL, D_MODEL)),
            full((1, D_MODEL)), full((1, D_MODEL)),
            full((A_GROUPS, CHUNK, CHUNK)), full((CHUNK, A_GROUPS)),
            full((D_MODEL, D_MODEL)),
        ],
        out_specs=pl.BlockSpec((tm, D_MODEL), lambda i: (i, 0)),
        out_shape=jax.ShapeDtypeStruct((n_tok, D_MODEL), BF16),
        scratch_shapes=[pltpu.VMEM((tm, D_MODEL), BF16), pltpu.VMEM((tm, D_MODEL), BF16)],
        compiler_params=pltpu.CompilerParams(
            dimension_semantics=("arbitrary",), vmem_limit_bytes=VMEM_LIMIT_BYTES),
        name="mixer_a",
    )(x2, wu, wv, wz, wg, lng, lnb, ws, bst, wba)


def _qkv_kernel(x_ref, cos_ref, sin_ref, wq_ref, wk_ref, wv_ref, wzb_ref, wgb_ref,
                q_ref, k_ref, vt_ref, szb_ref, sgb_ref, *, q_scale):
    xb = x_ref[...].astype(BF16)
    cos = cos_ref[...]
    sin = sin_ref[...]

    def rope(t):
        return t * cos + pltpu.roll(t, QK_DIM, 1) * sin

    q = _dot(xb, wq_ref[...])
    for h in range(HEADS):
        hs = slice(h * HEAD_DIM, (h + 1) * HEAD_DIM)
        q_ref[h] = (rope(q[:, hs]) * q_scale).astype(BF16)
    k = _dot(xb, wk_ref[...])
    for h in range(HEADS):
        hs = slice(h * HEAD_DIM, (h + 1) * HEAD_DIM)
        k_ref[h] = rope(k[:, hs]).astype(BF16)
    v = _dot(xb, wv_ref[...])
    for h in range(HEADS):
        hs = slice(h * HEAD_DIM, (h + 1) * HEAD_DIM)
        vt_ref[h] = v[:, hs].T.astype(BF16)
    zb = _dot(xb, wzb_ref[...])
    szb_ref[...] = (zb * _sigmoid(zb)).astype(BF16)
    gb = _dot(xb, wgb_ref[...])
    sgb_ref[...] = _sigmoid(gb).astype(BF16)


def _qkv(x3, cos, sin, wq, wk, wv, wzb, wgb):
    bn, s_len, _ = x3.shape
    tm = ATTN_TILE
    nkb = s_len // tm
    wspec = pl.BlockSpec((D_MODEL, D_MODEL), lambda b, i: (0, 0))
    return pl.pallas_call(
        functools.partial(_qkv_kernel, q_scale=QK_DIM ** -0.5),
        grid=(bn, nkb),
        in_specs=[
            pl.BlockSpec((None, tm, D_MODEL), lambda b, i: (b, i, 0)),
            pl.BlockSpec((None, tm, HEAD_DIM), lambda b, i: (b, i, 0)),
            pl.BlockSpec((None, tm, HEAD_DIM), lambda b, i: (b, i, 0)),
            wspec, wspec, wspec, wspec, wspec,
        ],
        out_specs=[
            pl.BlockSpec((None, HEADS, tm, HEAD_DIM), lambda b, i: (b, 0, i, 0)),
            pl.BlockSpec((None, HEADS, tm, HEAD_DIM), lambda b, i: (b, 0, i, 0)),
            pl.BlockSpec((None, HEADS, None, HEAD_DIM, tm), lambda b, i: (b, 0, i, 0, 0)),
            pl.BlockSpec((None, tm, D_MODEL), lambda b, i: (b, i, 0)),
            pl.BlockSpec((None, tm, D_MODEL), lambda b, i: (b, i, 0)),
        ],
        out_shape=[
            jax.ShapeDtypeStruct((bn, HEADS, s_len, HEAD_DIM), BF16),
            jax.ShapeDtypeStruct((bn, HEADS, s_len, HEAD_DIM), BF16),
            jax.ShapeDtypeStruct((bn, HEADS, nkb, HEAD_DIM, tm), BF16),
            jax.ShapeDtypeStruct((bn, s_len, D_MODEL), BF16),
            jax.ShapeDtypeStruct((bn, s_len, D_MODEL), BF16),
        ],
        compiler_params=pltpu.CompilerParams(
            dimension_semantics=("arbitrary", "arbitrary"), vmem_limit_bytes=VMEM_LIMIT_BYTES),
        name="qkv_proj",
    )(x3, cos, sin, wq, wk, wv, wzb, wgb)


def _attn_kernel(lq1_ref, lk1_ref, lq2_ref, lk2_ref, g_ref, q_ref, k_ref, vt_ref, o_ref,
                 qs_ref, acc_ref, *, lam_init):
    tq = q_ref.shape[0]
    tk = tq
    qi = pl.program_id(2)

    q = q_ref[...]
    lane = lax.broadcasted_iota(jnp.int32, q.shape, 1)
    in_map0 = (lane & 32) == 0
    zero = jnp.zeros_like(q)
    qs_ref[0:tq, :] = jnp.where(in_map0, q, zero)
    qs_ref[tq:2 * tq, :] = jnp.where(in_map0, zero, q)
    acc_ref[...] = jnp.zeros_like(acc_ref)

    def step(kb, m, l, on_diagonal):
        kblk = k_ref[pl.ds(pl.multiple_of(kb * tk, tk), tk), :]
        st = lax.dot_general(kblk, qs_ref[...], (((1,), (1,)), ((), ())),
                             preferred_element_type=F32)
        if on_diagonal:
            krow = lax.broadcasted_iota(jnp.int32, st.shape, 0)
            qcol = lax.broadcasted_iota(jnp.int32, st.shape, 1)
            qcol = jnp.where(qcol >= tq, qcol - tq, qcol)
            st = jnp.where(krow <= qcol, st, NEG_BIG)
        m_new = jnp.maximum(m, jnp.max(st, axis=0, keepdims=True))
        alpha = jnp.exp(m - m_new)
        pt = jnp.exp(st - m_new)
        l_new = alpha * l + jnp.sum(pt, axis=0, keepdims=True)
        acc_ref[...] = alpha * acc_ref[...] + _dot(vt_ref[kb], pt.astype(BF16))
        return m_new, l_new

    m0 = jnp.full((1, 2 * tq), NEG_BIG, F32)
    l0 = jnp.zeros((1, 2 * tq), F32)
    m, l = lax.fori_loop(0, qi, lambda kb, c: step(kb, c[0], c[1], False), (m0, l0))
    m, l = step(qi, m, l, True)

    acc = acc_ref[...] / l
    lam = (jnp.exp(jnp.sum(lq1_ref[...] * lk1_ref[...], keepdims=True))
           - jnp.exp(jnp.sum(lq2_ref[...] * lk2_ref[...], keepdims=True)) + lam_init)
    ot = acc[:, 0:tq] - lam * acc[:, tq:2 * tq]
    ms = jnp.mean(ot * ot, axis=0, keepdims=True)
    ot = ot * lax.rsqrt(ms + RMS_EPS) * g_ref[...] * (1.0 - lam_init)
    o_ref[...] = ot.T.astype(o_ref.dtype)


def _attention(q, k, vt, lq1, lk1, lq2, lk2, g_col, lam_init):
    bn, _, s_len, _ = q.shape
    tq = ATTN_TILE
    nkb = s_len // tq
    vec = pl.BlockSpec((1, QK_DIM), lambda b, h, i: (0, 0))
    return pl.pallas_call(
        functools.partial(_attn_kernel, lam_init=lam_init),
        grid=(bn, HEADS, nkb),
        in_specs=[
            vec, vec, vec, vec,
            pl.BlockSpec((HEAD_DIM, 1), lambda b, h, i: (0, 0)),
            pl.BlockSpec((None, None, tq, HEAD_DIM), lambda b, h, i: (b, h, i, 0)),
            pl.BlockSpec((None, None, s_len, HEAD_DIM), lambda b, h, i: (b, h, 0, 0)),
            pl.BlockSpec((None, None, nkb, HEAD_DIM, tq), lambda b, h, i: (b, h, 0, 0, 0)),
        ],
        out_specs=pl.BlockSpec((None, tq, HEAD_DIM), lambda b, h, i: (b, i, h)),
        out_shape=jax.ShapeDtypeStruct((bn, s_len, HEADS * HEAD_DIM), BF16),
        scratch_shapes=[pltpu.VMEM((2 * tq, HEAD_DIM), BF16), pltpu.VMEM((HEAD_DIM, 2 * tq), F32)],
        compiler_params=pltpu.CompilerParams(
            dimension_semantics=("arbitrary", "arbitrary", "arbitrary"),
            vmem_limit_bytes=VMEM_LIMIT_BYTES),
        name="diff_attn",
    )(lq1, lk1, lq2, lk2, g_col, q, k, vt)


def _final_kernel(x_ref, p_ref, o_ref, szb_ref, sgb_ref, pa_ref, wbb_ref, wout_ref, wpg_ref,
                  wp_ref, lng_ref, lnb_ref, out_ref, *, alpha):
    x = x_ref[...]
    xb = x.astype(BF16)
    yb = (o_ref[...].astype(F32) * szb_ref[...].astype(F32)).astype(BF16)
    merged = pa_ref[...].astype(F32) + sgb_ref[...].astype(F32) * _dot(yb, wbb_ref[...])
    mix = _dot(merged.astype(BF16), wout_ref[...])
    ple = _sigmoid(_dot(xb, wpg_ref[...])) * _dot(p_ref[...].astype(BF16), wp_ref[...])
    y = alpha * x + mix + ple
    out_ref[...] = _layer_norm(y, lng_ref[...], lnb_ref[...]).astype(out_ref.dtype)


def _final(x2, p2, o2, szb2, sgb2, pa2, wbb, wout, wpg, wp, lng, lnb, alpha):
    n_tok = x2.shape[0]
    tm = TOKEN_TILE
    tile = lambda w: pl.BlockSpec((tm, w), lambda i: (i, 0))
    full = lambda shape: pl.BlockSpec(shape, lambda i: (0,) * len(shape))
    return pl.pallas_call(
        functools.partial(_final_kernel, alpha=alpha),
        grid=(n_tok // tm,),
        in_specs=[
            tile(D_MODEL), tile(PLE_DIM), tile(D_MODEL), tile(D_MODEL), tile(D_MODEL), tile(D_MODEL),
            full((D_MODEL, D_MODEL)), full((D_MODEL, D_MODEL)), full((D_MODEL, D_MODEL)),
            full((PLE_DIM, D_MODEL)), full((1, D_MODEL)), full((1, D_MODEL)),
        ],
        out_specs=tile(D_MODEL),
        out_shape=jax.ShapeDtypeStruct((n_tok, D_MODEL), x2.dtype),
        compiler_params=pltpu.CompilerParams(
            dimension_semantics=("arbitrary",), vmem_limit_bytes=VMEM_LIMIT_BYTES),
        name="merge_out",
    )(x2, p2, o2, szb2, sgb2, pa2, wbb, wout, wpg, wp, lng, lnb)


def _rope_tables(positions):
    inv_freq = ROPE_THETA ** (-jnp.arange(0, QK_DIM, 2, dtype=F32) / QK_DIM)
    ang = positions.astype(F32)[..., None] * inv_freq
    cos, sin = jnp.cos(ang), jnp.sin(ang)
    return (jnp.concatenate([cos, cos, cos, cos], axis=-1),
            jnp.concatenate([-sin, -sin, sin, sin], axis=-1))


def _head_interleave(w):
    d = w.shape[0]
    return w.reshape(d, HEADS, 2, 2, QK_DIM // 2).transpose(0, 1, 3, 2, 4).reshape(d, HEADS * HEAD_DIM)


def kernel(x, p, positions, w_in, a_ln_g, a_ln_b, a_w_s, a_b_s, b_lam_q1, b_lam_k1, b_lam_q2,
           b_lam_k2, b_subln_g, w_branch_a, w_branch_b, w_out, w_ple, w_ple_gate, ln_g, ln_b):
    bn, s_len, d = x.shape
    depth = w_in.shape[0]
    n_tok = bn * s_len
    alpha = (2 * depth) ** 0.25
    cos, sin = _rope_tables(positions)
    row = lambda v: v.reshape(1, -1).astype(F32)
    for i in range(depth):
        lam_init = 0.8 - 0.6 * math.exp(-0.3 * i)
        wu, wv, wz, wq, wk, wvb, wzb, wga, wgb = [
            w_in[i, :, j * D_MODEL:(j + 1) * D_MODEL] for j in range(9)]
        x2 = x.reshape(n_tok, d)
        pa = _mixer_a(x2, wu.astype(BF16), wv.astype(BF16), wz.astype(BF16), wga.astype(BF16),
                      row(a_ln_g[i]), row(a_ln_b[i]), a_w_s[i], a_b_s[i].T,
                      w_branch_a[i].astype(BF16))
        q, k, vt, szb, sgb = _qkv(x, cos, sin, _head_interleave(wq).astype(BF16),
                                  _head_interleave(wk).astype(BF16), wvb.astype(BF16),
                                  wzb.astype(BF16), wgb.astype(BF16))
        o = _attention(q, k, vt, row(b_lam_q1[i]), row(b_lam_k1[i]), row(b_lam_q2[i]),
                       row(b_lam_k2[i]), b_subln_g[i].reshape(HEAD_DIM, 1).astype(F32), lam_init)
        y = _final(x2, p[i].reshape(n_tok, PLE_DIM), o.reshape(n_tok, d), szb.reshape(n_tok, d),
                   sgb.reshape(n_tok, d), pa, w_branch_b[i].astype(BF16), w_out[i].astype(BF16),
                   w_ple_gate[i].astype(BF16), w_ple[i].astype(BF16), row(ln_g[i]), row(ln_b[i]),
                   alpha)
        x = y.reshape(bn, s_len, d)
    return x
```

```python
import functools
import math

import jax
import jax.numpy as jnp
from jax import lax
from jax.experimental import pallas as pl
from jax.experimental.pallas import tpu as pltpu

F32 = jnp.float32
BF16 = jnp.bfloat16

D_MODEL = 1024
PLE_DIM = 256
CHUNK = 128
A_GROUPS = 8
HEADS = 8
QK_DIM = 64
HEAD_DIM = 2 * QK_DIM
ROPE_THETA = 10000.0
LN_EPS = 1e-5
RMS_EPS = 1e-5
NEG_BIG = -1e30

TOKEN_TILE = 512
COL_TILE = 256
ATTN_TILE = 512
SCORE_COLS = 256
VMEM_LIMIT_BYTES = 48 * 1024 * 1024


def _dot(a, b):
    return jnp.dot(a, b, preferred_element_type=F32)


def _gelu(x):
    return 0.5 * x * (1.0 + lax.erf(x * math.sqrt(0.5)))


def _sigmoid(x):
    return jax.nn.sigmoid(x)


def _layer_norm(x, g, b):
    mu = jnp.mean(x, axis=-1, keepdims=True)
    xc = x - mu
    var = jnp.mean(xc * xc, axis=-1, keepdims=True)
    return xc * lax.rsqrt(var + LN_EPS) * g + b


def _mixer_a_kernel(x_ref, wu_ref, wv_ref, wz_ref, wg_ref, lng_ref, lnb_ref, ws_ref, bst_ref,
                    wba_ref, out_ref, vb_ref, ya_ref):
    tm = x_ref.shape[0]
    xb = x_ref[...].astype(BF16)
    va = _gelu(_dot(xb, wv_ref[...]))
    vb_ref[...] = _layer_norm(va, lng_ref[...], lnb_ref[...]).astype(BF16)

    row = lax.broadcasted_iota(jnp.int32, (CHUNK, CHUNK), 0)
    col = lax.broadcasted_iota(jnp.int32, (CHUNK, CHUNK), 1)
    causal = col <= row
    groups_per_block = COL_TILE // CHUNK
    for cb in range(D_MODEL // COL_TILE):
        cols = slice(cb * COL_TILE, (cb + 1) * COL_TILE)
        u = _gelu(_dot(xb, wu_ref[:, cols]))
        z = _dot(xb, wz_ref[:, cols])
        sz = z * _sigmoid(z)
        for gi in range(groups_per_block):
            g = cb * groups_per_block + gi
            gcols = slice(g * CHUNK, (g + 1) * CHUNK)
            lcols = slice(gi * CHUNK, (gi + 1) * CHUNK)
            ws_g = jnp.where(causal, ws_ref[g], 0.0).astype(BF16)
            bias = bst_ref[:, g:g + 1]
            for n in range(tm // CHUNK):
                rows = slice(n * CHUNK, (n + 1) * CHUNK)
                sa = _dot(ws_g, vb_ref[rows, gcols]) + bias
                ya_ref[rows, gcols] = (u[rows, lcols] * sa * sz[rows, lcols]).astype(BF16)

    pa = _dot(ya_ref[...], wba_ref[...])
    ga = _dot(xb, wg_ref[...])
    out_ref[...] = (_sigmoid(ga) * pa).astype(out_ref.dtype)


def _mixer_a(x2, wu, wv, wz, wg, lng, lnb, ws, bst, wba):
    n_tok = x2.shape[0]
    tm = TOKEN_TILE
    full = lambda shape: pl.BlockSpec(shape, lambda i: (0,) * len(shape))
    return pl.pallas_call(
        _mixer_a_kernel,
        grid=(n_tok // tm,),
        in_specs=[
            pl.BlockSpec((tm, D_MODEL), lambda i: (i, 0)),
            full((D_MODEL, D_MODEL)), full((D_MODEL, D_MODEL)), full((D_MODEL, D_MODEL)),
            full((D_MODEL, D_MODEL)),
            full((1, D_MODEL)), full((1, D_MODEL)),
            full((A_GROUPS, CHUNK, CHUNK)), full((CHUNK, A_GROUPS)),
            full((D_MODEL, D_MODEL)),
        ],
        out_specs=pl.BlockSpec((tm, D_MODEL), lambda i: (i, 0)),
        out_shape=jax.ShapeDtypeStruct((n_tok, D_MODEL), BF16),
        scratch_shapes=[pltpu.VMEM((tm, D_MODEL), BF16), pltpu.VMEM((tm, D_MODEL), BF16)],
        compiler_params=pltpu.CompilerParams(
            dimension_semantics=("arbitrary",), vmem_limit_bytes=VMEM_LIMIT_BYTES),
        name="mixer_a",
    )(x2, wu, wv, wz, wg, lng, lnb, ws, bst, wba)


def _qkv_kernel(x_ref, cos_ref, sin_ref, wq_ref, wk_ref, wv_ref, wzb_ref, wgb_ref,
                q_ref, k_ref, vt_ref, szb_ref, sgb_ref, *, q_scale):
    xb = x_ref[...].astype(BF16)
    cos = cos_ref[...]
    sin = sin_ref[...]

    def rope(t):
        return t * cos + pltpu.roll(t, QK_DIM, 1) * sin

    q = _dot(xb, wq_ref[...])
    for h in range(HEADS):
        hs = slice(h * HEAD_DIM, (h + 1) * HEAD_DIM)
        q_ref[h] = (rope(q[:, hs]) * q_scale).astype(BF16)
    k = _dot(xb, wk_ref[...])
    for h in range(HEADS):
        hs = slice(h * HEAD_DIM, (h + 1) * HEAD_DIM)
        k_ref[h] = rope(k[:, hs]).astype(BF16)
    v = _dot(xb, wv_ref[...])
    for h in range(HEADS):
        hs = slice(h * HEAD_DIM, (h + 1) * HEAD_DIM)
        vt_ref[h] = v[:, hs].T.astype(BF16)
    zb = _dot(xb, wzb_ref[...])
    szb_ref[...] = (zb * _sigmoid(zb)).astype(BF16)
    gb = _dot(xb, wgb_ref[...])
    sgb_ref[...] = _sigmoid(gb).astype(BF16)


def _qkv(x3, cos, sin, wq, wk, wv, wzb, wgb):
    bn, s_len, _ = x3.shape
    tm = ATTN_TILE
    nkb = s_len // tm
    wspec = pl.BlockSpec((D_MODEL, D_MODEL), lambda b, i: (0, 0))
    return pl.pallas_call(
        functools.partial(_qkv_kernel, q_scale=QK_DIM ** -0.5 * math.log2(math.e)),
        grid=(bn, nkb),
        in_specs=[
            pl.BlockSpec((None, tm, D_MODEL), lambda b, i: (b, i, 0)),
            pl.BlockSpec((None, tm, HEAD_DIM), lambda b, i: (b, i, 0)),
            pl.BlockSpec((None, tm, HEAD_DIM), lambda b, i: (b, i, 0)),
            wspec, wspec, wspec, wspec, wspec,
        ],
        out_specs=[
            pl.BlockSpec((None, HEADS, tm, HEAD_DIM), lambda b, i: (b, 0, i, 0)),
            pl.BlockSpec((None, HEADS, tm, HEAD_DIM), lambda b, i: (b, 0, i, 0)),
            pl.BlockSpec((None, HEADS, None, HEAD_DIM, tm), lambda b, i: (b, 0, i, 0, 0)),
            pl.BlockSpec((None, tm, D_MODEL), lambda b, i: (b, i, 0)),
            pl.BlockSpec((None, tm, D_MODEL), lambda b, i: (b, i, 0)),
        ],
        out_shape=[
            jax.ShapeDtypeStruct((bn, HEADS, s_len, HEAD_DIM), BF16),
            jax.ShapeDtypeStruct((bn, HEADS, s_len, HEAD_DIM), BF16),
            jax.ShapeDtypeStruct((bn, HEADS, nkb, HEAD_DIM, tm), BF16),
            jax.ShapeDtypeStruct((bn, s_len, D_MODEL), BF16),
            jax.ShapeDtypeStruct((bn, s_len, D_MODEL), BF16),
        ],
        compiler_params=pltpu.CompilerParams(
            dimension_semantics=("arbitrary", "arbitrary"), vmem_limit_bytes=VMEM_LIMIT_BYTES),
        name="qkv_proj",
    )(x3, cos, sin, wq, wk, wv, wzb, wgb)


def _attn_kernel(lq1_ref, lk1_ref, lq2_ref, lk2_ref, g_ref, q_ref, k_ref, vt_ref, o_ref,
                 qs_ref, sa_ref, sb_ref, m_ref, l_ref, acc_ref, *, lam_init):
    tq = q_ref.shape[0]
    tk = tq
    qi = pl.program_id(2)

    q = q_ref[...]
    lane = lax.broadcasted_iota(jnp.int32, q.shape, 1)
    in_map0 = (lane & 32) == 0
    zero = jnp.zeros_like(q)
    qs_ref[0:tq, :] = jnp.where(in_map0, q, zero)
    qs_ref[tq:2 * tq, :] = jnp.where(in_map0, zero, q)
    acc_ref[...] = jnp.zeros_like(acc_ref)
    m_ref[...] = jnp.full(m_ref.shape, NEG_BIG, F32)
    l_ref[...] = jnp.zeros_like(l_ref)

    def scores(kb, s_ref):
        kblk = k_ref[pl.ds(pl.multiple_of(kb * tk, tk), tk), :]
        s_ref[...] = lax.dot_general(kblk, qs_ref[...], (((1,), (1,)), ((), ())),
                                     preferred_element_type=F32)

    def softmax_pv(kb, s_ref, on_diagonal):
        vt = vt_ref[kb]
        for c in range(2 * tq // SCORE_COLS):
            cols = slice(c * SCORE_COLS, (c + 1) * SCORE_COLS)
            st = s_ref[:, cols]
            if on_diagonal:
                krow = lax.broadcasted_iota(jnp.int32, st.shape, 0)
                qrow = lax.broadcasted_iota(jnp.int32, st.shape, 1) + (c * SCORE_COLS) % tq
                st = jnp.where(krow <= qrow, st, NEG_BIG)
            m_old = m_ref[:, cols]
            m_new = jnp.maximum(m_old, jnp.max(st, axis=0, keepdims=True))
            alpha = jnp.exp2(m_old - m_new)
            pt = jnp.exp2(st - m_new)
            l_ref[:, cols] = alpha * l_ref[:, cols] + jnp.sum(pt, axis=0, keepdims=True)
            m_ref[:, cols] = m_new
            acc_ref[:, cols] = alpha * acc_ref[:, cols] + _dot(vt, pt.astype(BF16))

    scores(0, sa_ref)

    @pl.loop(0, qi // 2)
    def _(t):
        scores(2 * t + 1, sb_ref)
        softmax_pv(2 * t, sa_ref, False)
        scores(2 * t + 2, sa_ref)
        softmax_pv(2 * t + 1, sb_ref, False)

    @pl.when(qi % 2 == 0)
    def _():
        softmax_pv(qi, sa_ref, True)

    @pl.when(qi % 2 == 1)
    def _():
        scores(qi, sb_ref)
        softmax_pv(qi - 1, sa_ref, False)
        softmax_pv(qi, sb_ref, True)

    acc = acc_ref[...] / l_ref[...]
    lam = (jnp.exp(jnp.sum(lq1_ref[...] * lk1_ref[...], keepdims=True))
           - jnp.exp(jnp.sum(lq2_ref[...] * lk2_ref[...], keepdims=True)) + lam_init)
    ot = acc[:, 0:tq] - lam * acc[:, tq:2 * tq]
    ms = jnp.mean(ot * ot, axis=0, keepdims=True)
    ot = ot * lax.rsqrt(ms + RMS_EPS) * g_ref[...] * (1.0 - lam_init)
    o_ref[...] = ot.T.astype(o_ref.dtype)


def _attention(q, k, vt, lq1, lk1, lq2, lk2, g_col, lam_init):
    bn, _, s_len, _ = q.shape
    tq = ATTN_TILE
    nkb = s_len // tq
    vec = pl.BlockSpec((1, QK_DIM), lambda b, h, i: (0, 0))
    return pl.pallas_call(
        functools.partial(_attn_kernel, lam_init=lam_init),
        grid=(bn, HEADS, nkb),
        in_specs=[
            vec, vec, vec, vec,
            pl.BlockSpec((HEAD_DIM, 1), lambda b, h, i: (0, 0)),
            pl.BlockSpec((None, None, tq, HEAD_DIM), lambda b, h, i: (b, h, i, 0)),
            pl.BlockSpec((None, None, s_len, HEAD_DIM), lambda b, h, i: (b, h, 0, 0)),
            pl.BlockSpec((None, None, nkb, HEAD_DIM, tq), lambda b, h, i: (b, h, 0, 0, 0)),
        ],
        out_specs=pl.BlockSpec((None, tq, HEAD_DIM), lambda b, h, i: (b, i, h)),
        out_shape=jax.ShapeDtypeStruct((bn, s_len, HEADS * HEAD_DIM), BF16),
        scratch_shapes=[
            pltpu.VMEM((2 * tq, HEAD_DIM), BF16),
            pltpu.VMEM((tq, 2 * tq), F32), pltpu.VMEM((tq, 2 * tq), F32),
            pltpu.VMEM((1, 2 * tq), F32), pltpu.VMEM((1, 2 * tq), F32),
            pltpu.VMEM((HEAD_DIM, 2 * tq), F32),
        ],
        compiler_params=pltpu.CompilerParams(
            dimension_semantics=("arbitrary", "arbitrary", "arbitrary"),
            vmem_limit_bytes=VMEM_LIMIT_BYTES),
        name="diff_attn",
    )(lq1, lk1, lq2, lk2, g_col, q, k, vt)


def _final_kernel(x_ref, p_ref, o_ref, szb_ref, sgb_ref, pa_ref, wbb_ref, wout_ref, wpg_ref,
                  wp_ref, lng_ref, lnb_ref, out_ref, *, alpha):
    x = x_ref[...]
    xb = x.astype(BF16)
    yb = (o_ref[...].astype(F32) * szb_ref[...].astype(F32)).astype(BF16)
    merged = pa_ref[...].astype(F32) + sgb_ref[...].astype(F32) * _dot(yb, wbb_ref[...])
    mix = _dot(merged.astype(BF16), wout_ref[...])
    ple = _sigmoid(_dot(xb, wpg_ref[...])) * _dot(p_ref[...].astype(BF16), wp_ref[...])
    y = alpha * x + mix + ple
    out_ref[...] = _layer_norm(y, lng_ref[...], lnb_ref[...]).astype(out_ref.dtype)


def _final(x2, p2, o2, szb2, sgb2, pa2, wbb, wout, wpg, wp, lng, lnb, alpha):
    n_tok = x2.shape[0]
    tm = TOKEN_TILE
    tile = lambda w: pl.BlockSpec((tm, w), lambda i: (i, 0))
    full = lambda shape: pl.BlockSpec(shape, lambda i: (0,) * len(shape))
    return pl.pallas_call(
        functools.partial(_final_kernel, alpha=alpha),
        grid=(n_tok // tm,),
        in_specs=[
            tile(D_MODEL), tile(PLE_DIM), tile(D_MODEL), tile(D_MODEL), tile(D_MODEL), tile(D_MODEL),
            full((D_MODEL, D_MODEL)), full((D_MODEL, D_MODEL)), full((D_MODEL, D_MODEL)),
            full((PLE_DIM, D_MODEL)), full((1, D_MODEL)), full((1, D_MODEL)),
        ],
        out_specs=tile(D_MODEL),
        out_shape=jax.ShapeDtypeStruct((n_tok, D_MODEL), x2.dtype),
        compiler_params=pltpu.CompilerParams(
            dimension_semantics=("arbitrary",), vmem_limit_bytes=VMEM_LIMIT_BYTES),
        name="merge_out",
    )(x2, p2, o2, szb2, sgb2, pa2, wbb, wout, wpg, wp, lng, lnb)


def _rope_tables(positions):
    inv_freq = ROPE_THETA ** (-jnp.arange(0, QK_DIM, 2, dtype=F32) / QK_DIM)
    ang = positions.astype(F32)[..., None] * inv_freq
    cos, sin = jnp.cos(ang), jnp.sin(ang)
    return (jnp.concatenate([cos, cos, cos, cos], axis=-1),
            jnp.concatenate([-sin, -sin, sin, sin], axis=-1))


def _head_interleave(w):
    d = w.shape[0]
    return w.reshape(d, HEADS, 2, 2, QK_DIM // 2).transpose(0, 1, 3, 2, 4).reshape(d, HEADS * HEAD_DIM)


def kernel(x, p, positions, w_in, a_ln_g, a_ln_b, a_w_s, a_b_s, b_lam_q1, b_lam_k1, b_lam_q2,
           b_lam_k2, b_subln_g, w_branch_a, w_branch_b, w_out, w_ple, w_ple_gate, ln_g, ln_b):
    bn, s_len, d = x.shape
    depth = w_in.shape[0]
    n_tok = bn * s_len
    alpha = (2 * depth) ** 0.25
    cos, sin = _rope_tables(positions)
    row = lambda v: v.reshape(1, -1).astype(F32)
    for i in range(depth):
        lam_init = 0.8 - 0.6 * math.exp(-0.3 * i)
        wu, wv, wz, wq, wk, wvb, wzb, wga, wgb = [
            w_in[i, :, j * D_MODEL:(j + 1) * D_MODEL] for j in range(9)]
        x2 = x.reshape(n_tok, d)
        pa = _mixer_a(x2, wu.astype(BF16), wv.astype(BF16), wz.astype(BF16), wga.astype(BF16),
                      row(a_ln_g[i]), row(a_ln_b[i]), a_w_s[i], a_b_s[i].T,
                      w_branch_a[i].astype(BF16))
        q, k, vt, szb, sgb = _qkv(x, cos, sin, _head_interleave(wq).astype(BF16),
                                  _head_interleave(wk).astype(BF16), wvb.astype(BF16),
                                  wzb.astype(BF16), wgb.astype(BF16))
        o = _attention(q, k, vt, row(b_lam_q1[i]), row(b_lam_k1[i]), row(b_lam_q2[i]),
                       row(b_lam_k2[i]), b_subln_g[i].reshape(HEAD_DIM, 1).astype(F32), lam_init)
        y = _final(x2, p[i].reshape(n_tok, PLE_DIM), o.reshape(n_tok, d), szb.reshape(n_tok, d),
                   sgb.reshape(n_tok, d), pa, w_branch_b[i].astype(BF16), w_out[i].astype(BF16),
                   w_ple_gate[i].astype(BF16), w_ple[i].astype(BF16), row(ln_g[i]), row(ln_b[i]),
                   alpha)
        x = y.reshape(bn, s_len, d)
    return x
```

```python
import functools
import math

import jax
import jax.numpy as jnp
from jax import lax
from jax.experimental import pallas as pl
from jax.experimental.pallas import tpu as pltpu

F32 = jnp.float32
BF16 = jnp.bfloat16

D_MODEL = 1024
PLE_DIM = 256
CHUNK = 128
A_GROUPS = 8
HEADS = 8
QK_DIM = 64
HEAD_DIM = 2 * QK_DIM
ROPE_THETA = 10000.0
LN_EPS = 1e-5
RMS_EPS = 1e-5
NEG_BIG = -1e30

TOKEN_TILE = 512
COL_TILE = 256
ATTN_TILE = 512
SCORE_COLS = 256
FULL_UNROLL = 4
VMEM_LIMIT_BYTES = 48 * 1024 * 1024
ATTN_VMEM_LIMIT_BYTES = 56 * 1024 * 1024


def _dot(a, b):
    return jnp.dot(a, b, preferred_element_type=F32)


def _aligned(index, multiple):
    return index if isinstance(index, int) else pl.multiple_of(index, multiple)


def _gelu(x):
    return 0.5 * x * (1.0 + lax.erf(x * math.sqrt(0.5)))


def _sigmoid(x):
    return jax.nn.sigmoid(x)


def _layer_norm(x, g, b):
    mu = jnp.mean(x, axis=-1, keepdims=True)
    xc = x - mu
    var = jnp.mean(xc * xc, axis=-1, keepdims=True)
    return xc * lax.rsqrt(var + LN_EPS) * g + b


def _mixer_a_kernel(x_ref, wu_ref, wv_ref, wz_ref, wg_ref, lng_ref, lnb_ref, ws_ref, bst_ref,
                    wba_ref, out_ref, vb_ref, ya_ref):
    tm = x_ref.shape[0]
    xb = x_ref[...].astype(BF16)
    va = _gelu(_dot(xb, wv_ref[...]))
    vb_ref[...] = _layer_norm(va, lng_ref[...], lnb_ref[...]).astype(BF16)

    row = lax.broadcasted_iota(jnp.int32, (CHUNK, CHUNK), 0)
    col = lax.broadcasted_iota(jnp.int32, (CHUNK, CHUNK), 1)
    causal = col <= row
    groups_per_block = COL_TILE // CHUNK
    for cb in range(D_MODEL // COL_TILE):
        cols = slice(cb * COL_TILE, (cb + 1) * COL_TILE)
        u = _gelu(_dot(xb, wu_ref[:, cols]))
        z = _dot(xb, wz_ref[:, cols])
        sz = z * _sigmoid(z)
        for gi in range(groups_per_block):
            g = cb * groups_per_block + gi
            gcols = slice(g * CHUNK, (g + 1) * CHUNK)
            lcols = slice(gi * CHUNK, (gi + 1) * CHUNK)
            ws_g = jnp.where(causal, ws_ref[g], 0.0).astype(BF16)
            bias = bst_ref[:, g:g + 1]
            for n in range(tm // CHUNK):
                rows = slice(n * CHUNK, (n + 1) * CHUNK)
                sa = _dot(ws_g, vb_ref[rows, gcols]) + bias
                ya_ref[rows, gcols] = (u[rows, lcols] * sa * sz[rows, lcols]).astype(BF16)

    pa = _dot(ya_ref[...], wba_ref[...])
    ga = _dot(xb, wg_ref[...])
    out_ref[...] = (_sigmoid(ga) * pa).astype(out_ref.dtype)


def _mixer_a(x2, wu, wv, wz, wg, lng, lnb, ws, bst, wba):
    n_tok = x2.shape[0]
    tm = TOKEN_TILE
    full = lambda shape: pl.BlockSpec(shape, lambda i: (0,) * len(shape))
    return pl.pallas_call(
        _mixer_a_kernel,
        grid=(n_tok // tm,),
        in_specs=[
            pl.BlockSpec((tm, D_MODEL), lambda i: (i, 0)),
            full((D_MODEL, D_MODEL)), full((D_MODEL, D_MODEL)), full((D_MODEL, D_MODEL)),
            full((D_MODEL, D_MODEL)),
            full((1, D_MODEL)), full((1, D_MODEL)),
            full((A_GROUPS, CHUNK, CHUNK)), full((CHUNK, A_GROUPS)),
            full((D_MODEL, D_MODEL)),
        ],
        out_specs=pl.BlockSpec((tm, D_MODEL), lambda i: (i, 0)),
        out_shape=jax.ShapeDtypeStruct((n_tok, D_MODEL), BF16),
        scratch_shapes=[pltpu.VMEM((tm, D_MODEL), BF16), pltpu.VMEM((tm, D_MODEL), BF16)],
        compiler_params=pltpu.CompilerParams(
            dimension_semantics=("arbitrary",), vmem_limit_bytes=VMEM_LIMIT_BYTES),
        name="mixer_a",
    )(x2, wu, wv, wz, wg, lng, lnb, ws, bst, wba)


def _qkv_kernel(x_ref, cos_ref, sin_ref, wq_ref, wk_ref, wv_ref, wzb_ref, wgb_ref,
                q_ref, k0_ref, k1_ref, vt_ref, szb_ref, sgb_ref, *, q_scale):
    xb = x_ref[...].astype(BF16)
    cos = cos_ref[...]
    sin = sin_ref[...]
    lane = lax.broadcasted_iota(jnp.int32, cos.shape, 1)
    in_map0 = (lane & (QK_DIM // 2)) == 0

    def rope(t):
        return t * cos + pltpu.roll(t, QK_DIM, 1) * sin

    q = _dot(xb, wq_ref[...])
    for h in range(HEADS):
        hs = slice(h * HEAD_DIM, (h + 1) * HEAD_DIM)
        q_ref[h] = (rope(q[:, hs]) * q_scale).astype(BF16)
    k = _dot(xb, wk_ref[...])
    for h in range(HEADS):
        hs = slice(h * HEAD_DIM, (h + 1) * HEAD_DIM)
        kr = rope(k[:, hs])
        k0_ref[h] = jnp.where(in_map0, kr, 0.0).astype(BF16)
        k1_ref[h] = jnp.where(in_map0, 0.0, kr).astype(BF16)
    v = _dot(xb, wv_ref[...])
    for h in range(HEADS):
        hs = slice(h * HEAD_DIM, (h + 1) * HEAD_DIM)
        vt_ref[h] = v[:, hs].T.astype(BF16)
    zb = _dot(xb, wzb_ref[...])
    szb_ref[...] = (zb * _sigmoid(zb)).astype(BF16)
    gb = _dot(xb, wgb_ref[...])
    sgb_ref[...] = _sigmoid(gb).astype(BF16)


def _qkv(x3, cos, sin, wq, wk, wv, wzb, wgb):
    bn, s_len, _ = x3.shape
    tm = ATTN_TILE
    nkb = s_len // tm
    wspec = pl.BlockSpec((D_MODEL, D_MODEL), lambda b, i: (0, 0))
    return pl.pallas_call(
        functools.partial(_qkv_kernel, q_scale=QK_DIM ** -0.5 * math.log2(math.e)),
        grid=(bn, nkb),
        in_specs=[
            pl.BlockSpec((None, tm, D_MODEL), lambda b, i: (b, i, 0)),
            pl.BlockSpec((None, tm, HEAD_DIM), lambda b, i: (b, i, 0)),
            pl.BlockSpec((None, tm, HEAD_DIM), lambda b, i: (b, i, 0)),
            wspec, wspec, wspec, wspec, wspec,
        ],
        out_specs=[
            pl.BlockSpec((None, HEADS, tm, HEAD_DIM), lambda b, i: (b, 0, i, 0)),
            pl.BlockSpec((None, HEADS, tm, HEAD_DIM), lambda b, i: (b, 0, i, 0)),
            pl.BlockSpec((None, HEADS, tm, HEAD_DIM), lambda b, i: (b, 0, i, 0)),
            pl.BlockSpec((None, HEADS, None, HEAD_DIM, tm), lambda b, i: (b, 0, i, 0, 0)),
            pl.BlockSpec((None, tm, D_MODEL), lambda b, i: (b, i, 0)),
            pl.BlockSpec((None, tm, D_MODEL), lambda b, i: (b, i, 0)),
        ],
        out_shape=[
            jax.ShapeDtypeStruct((bn, HEADS, s_len, HEAD_DIM), BF16),
            jax.ShapeDtypeStruct((bn, HEADS, s_len, HEAD_DIM), BF16),
            jax.ShapeDtypeStruct((bn, HEADS, s_len, HEAD_DIM), BF16),
            jax.ShapeDtypeStruct((bn, HEADS, nkb, HEAD_DIM, tm), BF16),
            jax.ShapeDtypeStruct((bn, s_len, D_MODEL), BF16),
            jax.ShapeDtypeStruct((bn, s_len, D_MODEL), BF16),
        ],
        compiler_params=pltpu.CompilerParams(
            dimension_semantics=("arbitrary", "arbitrary"), vmem_limit_bytes=VMEM_LIMIT_BYTES),
        name="qkv_proj",
    )(x3, cos, sin, wq, wk, wv, wzb, wgb)


def _attn_kernel(lq1_ref, lk1_ref, lq2_ref, lk2_ref, g_ref, q_ref, k0_ref, k1_ref, vt_ref, o_ref,
                 sa_ref, sb_ref, bma_ref, bmb_ref, m_ref, l_ref, acc_ref, *, lam_init):
    tq = ATTN_TILE
    tk = ATTN_TILE
    n_tiles = q_ref.shape[0] // tq
    n_streams = 2 * tq // SCORE_COLS
    per_map = n_streams // 2

    m_ref[...] = jnp.full(m_ref.shape, NEG_BIG, F32)
    l_ref[...] = jnp.zeros_like(l_ref)
    acc_ref[...] = jnp.zeros_like(acc_ref)
    lam = (jnp.exp(jnp.sum(lq1_ref[...] * lk1_ref[...], keepdims=True))
           - jnp.exp(jnp.sum(lq2_ref[...] * lk2_ref[...], keepdims=True)) + lam_init)

    def scores(qi, kb, s_ref, bm_ref, on_diagonal):
        q0 = _aligned(qi * tq, tq)
        k_start = _aligned(kb * tk, tk)
        for c in range(n_streams):
            qoff = (c % per_map) * SCORE_COLS
            rows = min(tk, qoff + SCORE_COLS) if on_diagonal else tk
            k_ref = k0_ref if c < per_map else k1_ref
            s = lax.dot_general(k_ref[pl.ds(k_start, rows), :],
                                q_ref[pl.ds(q0 + qoff, SCORE_COLS), :],
                                (((1,), (1,)), ((), ())), preferred_element_type=F32)
            s_ref[c, 0:rows, :] = s
            if not on_diagonal:
                bm_ref[c] = jnp.max(s, axis=0, keepdims=True)

    def softmax_pv(qi, kb, s_ref, bm_ref, on_diagonal):
        for c in range(n_streams):
            qoff = (c % per_map) * SCORE_COLS
            rows = min(tk, qoff + SCORE_COLS) if on_diagonal else tk
            st = s_ref[c, 0:rows, :]
            if on_diagonal:
                krow = lax.broadcasted_iota(jnp.int32, st.shape, 0)
                qcol = lax.broadcasted_iota(jnp.int32, st.shape, 1) + qoff
                st = jnp.where(krow <= qcol, st, NEG_BIG)
                bm = jnp.max(st, axis=0, keepdims=True)
            else:
                bm = bm_ref[c]
            m_old = m_ref[qi, c]
            m_new = jnp.maximum(m_old, bm)
            alpha = jnp.exp2(m_old - m_new)
            pt = jnp.exp2(st - m_new)
            l_ref[qi, c] = alpha * l_ref[qi, c] + jnp.sum(pt, axis=0, keepdims=True)
            m_ref[qi, c] = m_new
            acc_ref[qi, c] = alpha * acc_ref[qi, c] + _dot(vt_ref[kb, :, 0:rows], pt.astype(BF16))

    def finalize(qi):
        for j in range(per_map):
            o0 = acc_ref[qi, j] / l_ref[qi, j]
            o1 = acc_ref[qi, per_map + j] / l_ref[qi, per_map + j]
            ot = o0 - lam * o1
            ms = jnp.mean(ot * ot, axis=0, keepdims=True)
            ot = ot * lax.rsqrt(ms + RMS_EPS) * g_ref[...] * (1.0 - lam_init)
            row0 = _aligned(qi * tq + j * SCORE_COLS, SCORE_COLS)
            o_ref[pl.ds(row0, SCORE_COLS), :] = ot.T.astype(o_ref.dtype)

    bufs = ((sa_ref, bma_ref), (sb_ref, bmb_ref))

    n_full = n_tiles * (n_tiles - 1) // 2
    assert n_full % FULL_UNROLL == 0 and FULL_UNROLL % 2 == 0 and n_tiles % 2 == 0

    def advance(qi, kb):
        wrap = kb + 1 >= qi
        return jnp.where(wrap, qi + 1, qi), jnp.where(wrap, 0, kb + 1)

    scores(1, 0, sa_ref, bma_ref, False)

    def full_body(_, carry):
        qi, kb = carry
        for step in range(FULL_UNROLL):
            cur, nxt = bufs[step % 2], bufs[(step + 1) % 2]
            nqi, nkb = advance(qi, kb)
            scores(jnp.minimum(nqi, n_tiles - 1), nkb, nxt[0], nxt[1], False)
            softmax_pv(qi, kb, cur[0], cur[1], False)
            qi, kb = nqi, nkb
        return qi, kb

    lax.fori_loop(0, n_full // FULL_UNROLL, full_body, (jnp.int32(1), jnp.int32(0)))

    scores(0, 0, sa_ref, bma_ref, True)

    def diag_body(u, carry):
        for step in range(2):
            cur, nxt = bufs[step % 2], bufs[(step + 1) % 2]
            qi = 2 * u + step
            nqi = jnp.minimum(qi + 1, n_tiles - 1)
            scores(nqi, nqi, nxt[0], nxt[1], True)
            softmax_pv(qi, qi, cur[0], cur[1], True)
            finalize(qi)
        return carry

    lax.fori_loop(0, n_tiles // 2, diag_body, 0)


def _attention(q, k0, k1, vt, lq1, lk1, lq2, lk2, g_col, lam_init):
    bn, _, s_len, _ = q.shape
    tq = ATTN_TILE
    n_tiles = s_len // tq
    n_streams = 2 * tq // SCORE_COLS
    vec = pl.BlockSpec((1, QK_DIM), lambda b, h: (0, 0))
    seq = pl.BlockSpec((None, None, s_len, HEAD_DIM), lambda b, h: (b, h, 0, 0))
    return pl.pallas_call(
        functools.partial(_attn_kernel, lam_init=lam_init),
        grid=(bn, HEADS),
        in_specs=[
            vec, vec, vec, vec,
            pl.BlockSpec((HEAD_DIM, 1), lambda b, h: (0, 0)),
            seq, seq, seq,
            pl.BlockSpec((None, None, n_tiles, HEAD_DIM, tq), lambda b, h: (b, h, 0, 0, 0)),
        ],
        out_specs=pl.BlockSpec((None, s_len, HEAD_DIM), lambda b, h: (b, 0, h)),
        out_shape=jax.ShapeDtypeStruct((bn, s_len, HEADS * HEAD_DIM), BF16),
        scratch_shapes=[
            pltpu.VMEM((n_streams, tq, SCORE_COLS), F32),
            pltpu.VMEM((n_streams, tq, SCORE_COLS), F32),
            pltpu.VMEM((n_streams, 1, SCORE_COLS), F32),
            pltpu.VMEM((n_streams, 1, SCORE_COLS), F32),
            pltpu.VMEM((n_tiles, n_streams, 1, SCORE_COLS), F32),
            pltpu.VMEM((n_tiles, n_streams, 1, SCORE_COLS), F32),
            pltpu.VMEM((n_tiles, n_streams, HEAD_DIM, SCORE_COLS), F32),
        ],
        compiler_params=pltpu.CompilerParams(
            dimension_semantics=("arbitrary", "arbitrary"),
            vmem_limit_bytes=ATTN_VMEM_LIMIT_BYTES),
        name="diff_attn",
    )(lq1, lk1, lq2, lk2, g_col, q, k0, k1, vt)


def _final_kernel(x_ref, p_ref, o_ref, szb_ref, sgb_ref, pa_ref, wbb_ref, wout_ref, wpg_ref,
                  wp_ref, lng_ref, lnb_ref, out_ref, *, alpha):
    x = x_ref[...]
    xb = x.astype(BF16)
    yb = (o_ref[...].astype(F32) * szb_ref[...].astype(F32)).astype(BF16)
    merged = pa_ref[...].astype(F32) + sgb_ref[...].astype(F32) * _dot(yb, wbb_ref[...])
    mix = _dot(merged.astype(BF16), wout_ref[...])
    ple = _sigmoid(_dot(xb, wpg_ref[...])) * _dot(p_ref[...].astype(BF16), wp_ref[...])
    y = alpha * x + mix + ple
    out_ref[...] = _layer_norm(y, lng_ref[...], lnb_ref[...]).astype(out_ref.dtype)


def _final(x2, p2, o2, szb2, sgb2, pa2, wbb, wout, wpg, wp, lng, lnb, alpha):
    n_tok = x2.shape[0]
    tm = TOKEN_TILE
    tile = lambda w: pl.BlockSpec((tm, w), lambda i: (i, 0))
    full = lambda shape: pl.BlockSpec(shape, lambda i: (0,) * len(shape))
    return pl.pallas_call(
        functools.partial(_final_kernel, alpha=alpha),
        grid=(n_tok // tm,),
        in_specs=[
            tile(D_MODEL), tile(PLE_DIM), tile(D_MODEL), tile(D_MODEL), tile(D_MODEL), tile(D_MODEL),
            full((D_MODEL, D_MODEL)), full((D_MODEL, D_MODEL)), full((D_MODEL, D_MODEL)),
            full((PLE_DIM, D_MODEL)), full((1, D_MODEL)), full((1, D_MODEL)),
        ],
        out_specs=tile(D_MODEL),
        out_shape=jax.ShapeDtypeStruct((n_tok, D_MODEL), x2.dtype),
        compiler_params=pltpu.CompilerParams(
            dimension_semantics=("arbitrary",), vmem_limit_bytes=VMEM_LIMIT_BYTES),
        name="merge_out",
    )(x2, p2, o2, szb2, sgb2, pa2, wbb, wout, wpg, wp, lng, lnb)


def _rope_tables(positions):
    inv_freq = ROPE_THETA ** (-jnp.arange(0, QK_DIM, 2, dtype=F32) / QK_DIM)
    ang = positions.astype(F32)[..., None] * inv_freq
    cos, sin = jnp.cos(ang), jnp.sin(ang)
    return (jnp.concatenate([cos, cos, cos, cos], axis=-1),
            jnp.concatenate([-sin, -sin, sin, sin], axis=-1))


def _head_interleave(w):
    d = w.shape[0]
    return w.reshape(d, HEADS, 2, 2, QK_DIM // 2).transpose(0, 1, 3, 2, 4).reshape(d, HEADS * HEAD_DIM)


def kernel(x, p, positions, w_in, a_ln_g, a_ln_b, a_w_s, a_b_s, b_lam_q1, b_lam_k1, b_lam_q2,
           b_lam_k2, b_subln_g, w_branch_a, w_branch_b, w_out, w_ple, w_ple_gate, ln_g, ln_b):
    bn, s_len, d = x.shape
    depth = w_in.shape[0]
    n_tok = bn * s_len
    alpha = (2 * depth) ** 0.25
    cos, sin = _rope_tables(positions)
    row = lambda v: v.reshape(1, -1).astype(F32)
    for i in range(depth):
        lam_init = 0.8 - 0.6 * math.exp(-0.3 * i)
        wu, wv, wz, wq, wk, wvb, wzb, wga, wgb = [
            w_in[i, :, j * D_MODEL:(j + 1) * D_MODEL] for j in range(9)]
        x2 = x.reshape(n_tok, d)
        pa = _mixer_a(x2, wu.astype(BF16), wv.astype(BF16), wz.astype(BF16), wga.astype(BF16),
                      row(a_ln_g[i]), row(a_ln_b[i]), a_w_s[i], a_b_s[i].T,
                      w_branch_a[i].astype(BF16))
        q, k0, k1, vt, szb, sgb = _qkv(x, cos, sin, _head_interleave(wq).astype(BF16),
                                  _head_interleave(wk).astype(BF16), wvb.astype(BF16),
                                  wzb.astype(BF16), wgb.astype(BF16))
        o = _attention(q, k0, k1, vt, row(b_lam_q1[i]), row(b_lam_k1[i]), row(b_lam_q2[i]),
                       row(b_lam_k2[i]), b_subln_g[i].reshape(HEAD_DIM, 1).astype(F32), lam_init)
        y = _final(x2, p[i].reshape(n_tok, PLE_DIM), o.reshape(n_tok, d), szb.reshape(n_tok, d),
                   sgb.reshape(n_tok, d), pa, w_branch_b[i].astype(BF16), w_out[i].astype(BF16),
                   w_ple_gate[i].astype(BF16), w_ple[i].astype(BF16), row(ln_g[i]), row(ln_b[i]),
                   alpha)
        x = y.reshape(bn, s_len, d)
    return x
```

```python
import functools
import math

import jax
import jax.numpy as jnp
from jax import lax
from jax.experimental import pallas as pl
from jax.experimental.pallas import tpu as pltpu

F32 = jnp.float32
BF16 = jnp.bfloat16

D_MODEL = 1024
PLE_DIM = 256
CHUNK = 128
A_GROUPS = 8
HEADS = 8
QK_DIM = 64
HEAD_DIM = 2 * QK_DIM
ROPE_THETA = 10000.0
LN_EPS = 1e-5
RMS_EPS = 1e-5
NEG_BIG = -1e30

TOKEN_TILE = 512
COL_TILE = 256
ATTN_TILE = 512
SCORE_COLS = 256
V_ROWS = HEAD_DIM + 16
FULL_UNROLL = 8
VMEM_LIMIT_BYTES = 48 * 1024 * 1024
ATTN_VMEM_LIMIT_BYTES = 56 * 1024 * 1024


def _dot(a, b):
    return jnp.dot(a, b, preferred_element_type=F32)


def _aligned(index, multiple):
    return index if isinstance(index, int) else pl.multiple_of(index, multiple)


def _gelu(x):
    return 0.5 * x * (1.0 + lax.erf(x * math.sqrt(0.5)))


def _sigmoid(x):
    return jax.nn.sigmoid(x)


def _layer_norm(x, g, b):
    mu = jnp.mean(x, axis=-1, keepdims=True)
    xc = x - mu
    var = jnp.mean(xc * xc, axis=-1, keepdims=True)
    return xc * lax.rsqrt(var + LN_EPS) * g + b


def _mixer_a_kernel(x_ref, wu_ref, wv_ref, wz_ref, wg_ref, lng_ref, lnb_ref, ws_ref, bst_ref,
                    wba_ref, out_ref, vb_ref, ya_ref):
    tm = x_ref.shape[0]
    xb = x_ref[...].astype(BF16)
    va = _gelu(_dot(xb, wv_ref[...]))
    vb_ref[...] = _layer_norm(va, lng_ref[...], lnb_ref[...]).astype(BF16)

    row = lax.broadcasted_iota(jnp.int32, (CHUNK, CHUNK), 0)
    col = lax.broadcasted_iota(jnp.int32, (CHUNK, CHUNK), 1)
    causal = col <= row
    groups_per_block = COL_TILE // CHUNK
    for cb in range(D_MODEL // COL_TILE):
        cols = slice(cb * COL_TILE, (cb + 1) * COL_TILE)
        u = _gelu(_dot(xb, wu_ref[:, cols]))
        z = _dot(xb, wz_ref[:, cols])
        sz = z * _sigmoid(z)
        for gi in range(groups_per_block):
            g = cb * groups_per_block + gi
            gcols = slice(g * CHUNK, (g + 1) * CHUNK)
            lcols = slice(gi * CHUNK, (gi + 1) * CHUNK)
            ws_g = jnp.where(causal, ws_ref[g], 0.0).astype(BF16)
            bias = bst_ref[:, g:g + 1]
            for n in range(tm // CHUNK):
                rows = slice(n * CHUNK, (n + 1) * CHUNK)
                sa = _dot(ws_g, vb_ref[rows, gcols]) + bias
                ya_ref[rows, gcols] = (u[rows, lcols] * sa * sz[rows, lcols]).astype(BF16)

    pa = _dot(ya_ref[...], wba_ref[...])
    ga = _dot(xb, wg_ref[...])
    out_ref[...] = (_sigmoid(ga) * pa).astype(out_ref.dtype)


def _mixer_a(x2, wu, wv, wz, wg, lng, lnb, ws, bst, wba):
    n_tok = x2.shape[0]
    tm = TOKEN_TILE
    full = lambda shape: pl.BlockSpec(shape, lambda i: (0,) * len(shape))
    return pl.pallas_call(
        _mixer_a_kernel,
        grid=(n_tok // tm,),
        in_specs=[
            pl.BlockSpec((tm, D_MODEL), lambda i: (i, 0)),
            full((D_MODEL, D_MODEL)), full((D_MODEL, D_MODEL)), full((D_MODEL, D_MODEL)),
            full((D_MODEL, D_MODEL)),
            full((1, D_MODEL)), full((1, D_MODEL)),
            full((A_GROUPS, CHUNK, CHUNK)), full((CHUNK, A_GROUPS)),
            full((D_MODEL, D_MODEL)),
        ],
        out_specs=pl.BlockSpec((tm, D_MODEL), lambda i: (i, 0)),
        out_shape=jax.ShapeDtypeStruct((n_tok, D_MODEL), BF16),
        scratch_shapes=[pltpu.VMEM((tm, D_MODEL), BF16), pltpu.VMEM((tm, D_MODEL), BF16)],
        compiler_params=pltpu.CompilerParams(
            dimension_semantics=("arbitrary",), vmem_limit_bytes=VMEM_LIMIT_BYTES),
        name="mixer_a",
    )(x2, wu, wv, wz, wg, lng, lnb, ws, bst, wba)


def _qkv_kernel(x_ref, pos_ref, freq_ref, sign_ref, wq_ref, wk_ref, wv_ref, wzb_ref, wgb_ref,
                q_ref, k0_ref, k1_ref, vt_ref, szb_ref, sgb_ref, *, q_scale):
    xb = x_ref[...].astype(BF16)
    ang = pos_ref[...].astype(F32) * freq_ref[...]
    cos = jnp.cos(ang)
    sin = jnp.sin(ang) * sign_ref[...]
    lane = lax.broadcasted_iota(jnp.int32, cos.shape, 1)
    in_map0 = (lane & (QK_DIM // 2)) == 0

    def rope(t):
        return t * cos + pltpu.roll(t, QK_DIM, 1) * sin

    q = _dot(xb, wq_ref[...])
    for h in range(HEADS):
        hs = slice(h * HEAD_DIM, (h + 1) * HEAD_DIM)
        q_ref[h] = (rope(q[:, hs]) * q_scale).astype(BF16)
    k = _dot(xb, wk_ref[...])
    for h in range(HEADS):
        hs = slice(h * HEAD_DIM, (h + 1) * HEAD_DIM)
        kr = rope(k[:, hs])
        k0_ref[h] = jnp.where(in_map0, kr, 0.0).astype(BF16)
        k1_ref[h] = jnp.where(in_map0, 0.0, kr).astype(BF16)
    v = _dot(xb, wv_ref[...])
    for h in range(HEADS):
        hs = slice(h * HEAD_DIM, (h + 1) * HEAD_DIM)
        vt_ref[h, 0:HEAD_DIM, :] = v[:, hs].T.astype(BF16)
        vt_ref[h, HEAD_DIM:V_ROWS, :] = jnp.ones((V_ROWS - HEAD_DIM, v.shape[0]), BF16)
    zb = _dot(xb, wzb_ref[...])
    szb_ref[...] = (zb * _sigmoid(zb)).astype(BF16)
    gb = _dot(xb, wgb_ref[...])
    sgb_ref[...] = _sigmoid(gb).astype(BF16)


def _qkv(x3, pos3, freq, sign, wq, wk, wv, wzb, wgb):
    bn, s_len, _ = x3.shape
    tm = ATTN_TILE
    nkb = s_len // tm
    wspec = pl.BlockSpec((D_MODEL, D_MODEL), lambda b, i: (0, 0))
    lane_row = pl.BlockSpec((1, HEAD_DIM), lambda b, i: (0, 0))
    return pl.pallas_call(
        functools.partial(_qkv_kernel, q_scale=QK_DIM ** -0.5 * math.log2(math.e)),
        grid=(bn, nkb),
        in_specs=[
            pl.BlockSpec((None, tm, D_MODEL), lambda b, i: (b, i, 0)),
            pl.BlockSpec((None, tm, 1), lambda b, i: (b, i, 0)),
            lane_row, lane_row,
            wspec, wspec, wspec, wspec, wspec,
        ],
        out_specs=[
            pl.BlockSpec((None, HEADS, tm, HEAD_DIM), lambda b, i: (b, 0, i, 0)),
            pl.BlockSpec((None, HEADS, tm, HEAD_DIM), lambda b, i: (b, 0, i, 0)),
            pl.BlockSpec((None, HEADS, tm, HEAD_DIM), lambda b, i: (b, 0, i, 0)),
            pl.BlockSpec((None, HEADS, None, V_ROWS, tm), lambda b, i: (b, 0, i, 0, 0)),
            pl.BlockSpec((None, tm, D_MODEL), lambda b, i: (b, i, 0)),
            pl.BlockSpec((None, tm, D_MODEL), lambda b, i: (b, i, 0)),
        ],
        out_shape=[
            jax.ShapeDtypeStruct((bn, HEADS, s_len, HEAD_DIM), BF16),
            jax.ShapeDtypeStruct((bn, HEADS, s_len, HEAD_DIM), BF16),
            jax.ShapeDtypeStruct((bn, HEADS, s_len, HEAD_DIM), BF16),
            jax.ShapeDtypeStruct((bn, HEADS, nkb, V_ROWS, tm), BF16),
            jax.ShapeDtypeStruct((bn, s_len, D_MODEL), BF16),
            jax.ShapeDtypeStruct((bn, s_len, D_MODEL), BF16),
        ],
        compiler_params=pltpu.CompilerParams(
            dimension_semantics=("arbitrary", "arbitrary"), vmem_limit_bytes=VMEM_LIMIT_BYTES),
        name="qkv_proj",
    )(x3, pos3, freq, sign, wq, wk, wv, wzb, wgb)


def _attn_kernel(lq1_ref, lk1_ref, lq2_ref, lk2_ref, g_ref, q_ref, k0_ref, k1_ref, vt_ref, o_ref,
                 sa_ref, sb_ref, bma_ref, bmb_ref, m_ref, acc_ref, *, lam_init):
    tq = ATTN_TILE
    tk = ATTN_TILE
    n_tiles = q_ref.shape[0] // tq
    n_streams = 2 * tq // SCORE_COLS
    per_map = n_streams // 2

    m_ref[...] = jnp.full(m_ref.shape, NEG_BIG, F32)
    acc_ref[...] = jnp.zeros_like(acc_ref)
    lam = (jnp.exp(jnp.sum(lq1_ref[...] * lk1_ref[...], keepdims=True))
           - jnp.exp(jnp.sum(lq2_ref[...] * lk2_ref[...], keepdims=True)) + lam_init)

    def scores(qi, kb, s_ref, bm_ref, on_diagonal):
        q0 = _aligned(qi * tq, tq)
        k_start = _aligned(kb * tk, tk)
        for c in range(n_streams):
            qoff = (c % per_map) * SCORE_COLS
            rows = min(tk, qoff + SCORE_COLS) if on_diagonal else tk
            k_ref = k0_ref if c < per_map else k1_ref
            s = lax.dot_general(k_ref[pl.ds(k_start, rows), :],
                                q_ref[pl.ds(q0 + qoff, SCORE_COLS), :],
                                (((1,), (1,)), ((), ())), preferred_element_type=F32)
            s_ref[c, 0:rows, :] = s
            if not on_diagonal:
                bm_ref[c] = jnp.max(s, axis=0, keepdims=True)

    def softmax_pv(qi, kb, s_ref, bm_ref, on_diagonal):
        for c in range(n_streams):
            qoff = (c % per_map) * SCORE_COLS
            rows = min(tk, qoff + SCORE_COLS) if on_diagonal else tk
            st = s_ref[c, 0:rows, :]
            if on_diagonal:
                krow = lax.broadcasted_iota(jnp.int32, st.shape, 0)
                qcol = lax.broadcasted_iota(jnp.int32, st.shape, 1) + qoff
                st = jnp.where(krow <= qcol, st, NEG_BIG)
                bm = jnp.max(st, axis=0, keepdims=True)
            else:
                bm = bm_ref[c]
            m_old = m_ref[qi, c]
            m_new = jnp.maximum(m_old, bm)
            alpha = jnp.exp2(m_old - m_new)
            pt = jnp.exp2(st - m_new)
            m_ref[qi, c] = m_new
            acc_ref[qi, c] = alpha * acc_ref[qi, c] + _dot(vt_ref[kb, :, 0:rows], pt.astype(BF16))

    def finalize(qi):
        for j in range(per_map):
            a0 = acc_ref[qi, j]
            a1 = acc_ref[qi, per_map + j]
            ot = (a0[0:HEAD_DIM] / a0[HEAD_DIM:HEAD_DIM + 1]
                  - lam * (a1[0:HEAD_DIM] / a1[HEAD_DIM:HEAD_DIM + 1]))
            ms = jnp.mean(ot * ot, axis=0, keepdims=True)
            ot = ot * lax.rsqrt(ms + RMS_EPS) * g_ref[...] * (1.0 - lam_init)
            row0 = _aligned(qi * tq + j * SCORE_COLS, SCORE_COLS)
            o_ref[pl.ds(row0, SCORE_COLS), :] = ot.T.astype(o_ref.dtype)

    bufs = ((sa_ref, bma_ref), (sb_ref, bmb_ref))

    n_full = n_tiles * (n_tiles - 1) // 2
    assert n_full % FULL_UNROLL == 0 and FULL_UNROLL % 2 == 0 and n_tiles % 2 == 0

    def advance(qi, kb):
        wrap = kb + 1 >= qi
        return jnp.where(wrap, qi + 1, qi), jnp.where(wrap, 0, kb + 1)

    scores(1, 0, sa_ref, bma_ref, False)

    def full_body(_, carry):
        qi, kb = carry
        for step in range(FULL_UNROLL):
            cur, nxt = bufs[step % 2], bufs[(step + 1) % 2]
            nqi, nkb = advance(qi, kb)
            scores(jnp.minimum(nqi, n_tiles - 1), nkb, nxt[0], nxt[1], False)
            softmax_pv(qi, kb, cur[0], cur[1], False)
            qi, kb = nqi, nkb
        return qi, kb

    lax.fori_loop(0, n_full // FULL_UNROLL, full_body, (jnp.int32(1), jnp.int32(0)))

    scores(0, 0, sa_ref, bma_ref, True)

    def diag_body(u, carry):
        for step in range(2):
            cur, nxt = bufs[step % 2], bufs[(step + 1) % 2]
            qi = 2 * u + step
            nqi = jnp.minimum(qi + 1, n_tiles - 1)
            scores(nqi, nqi, nxt[0], nxt[1], True)
            softmax_pv(qi, qi, cur[0], cur[1], True)
            finalize(qi)
        return carry

    lax.fori_loop(0, n_tiles // 2, diag_body, 0)


def _attention(q, k0, k1, vt, lq1, lk1, lq2, lk2, g_col, lam_init):
    bn, _, s_len, _ = q.shape
    tq = ATTN_TILE
    n_tiles = s_len // tq
    n_streams = 2 * tq // SCORE_COLS
    vec = pl.BlockSpec((1, QK_DIM), lambda b, h: (0, 0))
    seq = pl.BlockSpec((None, None, s_len, HEAD_DIM), lambda b, h: (b, h, 0, 0))
    return pl.pallas_call(
        functools.partial(_attn_kernel, lam_init=lam_init),
        grid=(bn, HEADS),
        in_specs=[
            vec, vec, vec, vec,
            pl.BlockSpec((HEAD_DIM, 1), lambda b, h: (0, 0)),
            seq, seq, seq,
            pl.BlockSpec((None, None, n_tiles, V_ROWS, tq), lambda b, h: (b, h, 0, 0, 0)),
        ],
        out_specs=pl.BlockSpec((None, s_len, HEAD_DIM), lambda b, h: (b, 0, h)),
        out_shape=jax.ShapeDtypeStruct((bn, s_len, HEADS * HEAD_DIM), BF16),
        scratch_shapes=[
            pltpu.VMEM((n_streams, tq, SCORE_COLS), F32),
            pltpu.VMEM((n_streams, tq, SCORE_COLS), F32),
            pltpu.VMEM((n_streams, 1, SCORE_COLS), F32),
            pltpu.VMEM((n_streams, 1, SCORE_COLS), F32),
            pltpu.VMEM((n_tiles, n_streams, 1, SCORE_COLS), F32),
            pltpu.VMEM((n_tiles, n_streams, V_ROWS, SCORE_COLS), F32),
        ],
        compiler_params=pltpu.CompilerParams(
            dimension_semantics=("arbitrary", "arbitrary"),
            vmem_limit_bytes=ATTN_VMEM_LIMIT_BYTES),
        name="diff_attn",
    )(lq1, lk1, lq2, lk2, g_col, q, k0, k1, vt)


def _final_kernel(x_ref, p_ref, o_ref, szb_ref, sgb_ref, pa_ref, wbb_ref, wout_ref, wpg_ref,
                  wp_ref, lng_ref, lnb_ref, out_ref, *, alpha):
    x = x_ref[...]
    xb = x.astype(BF16)
    yb = (o_ref[...].astype(F32) * szb_ref[...].astype(F32)).astype(BF16)
    merged = pa_ref[...].astype(F32) + sgb_ref[...].astype(F32) * _dot(yb, wbb_ref[...])
    mix = _dot(merged.astype(BF16), wout_ref[...])
    ple = _sigmoid(_dot(xb, wpg_ref[...])) * _dot(p_ref[...].astype(BF16), wp_ref[...])
    y = alpha * x + mix + ple
    out_ref[...] = _layer_norm(y, lng_ref[...], lnb_ref[...]).astype(out_ref.dtype)


def _final(x2, p2, o2, szb2, sgb2, pa2, wbb, wout, wpg, wp, lng, lnb, alpha):
    n_tok = x2.shape[0]
    tm = TOKEN_TILE
    tile = lambda w: pl.BlockSpec((tm, w), lambda i: (i, 0))
    full = lambda shape: pl.BlockSpec(shape, lambda i: (0,) * len(shape))
    return pl.pallas_call(
        functools.partial(_final_kernel, alpha=alpha),
        grid=(n_tok // tm,),
        in_specs=[
            tile(D_MODEL), tile(PLE_DIM), tile(D_MODEL), tile(D_MODEL), tile(D_MODEL), tile(D_MODEL),
            full((D_MODEL, D_MODEL)), full((D_MODEL, D_MODEL)), full((D_MODEL, D_MODEL)),
            full((PLE_DIM, D_MODEL)), full((1, D_MODEL)), full((1, D_MODEL)),
        ],
        out_specs=tile(D_MODEL),
        out_shape=jax.ShapeDtypeStruct((n_tok, D_MODEL), x2.dtype),
        compiler_params=pltpu.CompilerParams(
            dimension_semantics=("arbitrary",), vmem_limit_bytes=VMEM_LIMIT_BYTES),
        name="merge_out",
    )(x2, p2, o2, szb2, sgb2, pa2, wbb, wout, wpg, wp, lng, lnb)


def _rope_rows():
    inv_freq = ROPE_THETA ** (-jnp.arange(0, QK_DIM, 2, dtype=F32) / QK_DIM)
    ones = jnp.ones_like(inv_freq)
    return (jnp.tile(inv_freq, 4).reshape(1, HEAD_DIM),
            jnp.concatenate([-ones, -ones, ones, ones]).reshape(1, HEAD_DIM))


def _head_interleave(w):
    d = w.shape[0]
    return w.reshape(d, HEADS, 2, 2, QK_DIM // 2).transpose(0, 1, 3, 2, 4).reshape(d, HEADS * HEAD_DIM)


def kernel(x, p, positions, w_in, a_ln_g, a_ln_b, a_w_s, a_b_s, b_lam_q1, b_lam_k1, b_lam_q2,
           b_lam_k2, b_subln_g, w_branch_a, w_branch_b, w_out, w_ple, w_ple_gate, ln_g, ln_b):
    bn, s_len, d = x.shape
    depth = w_in.shape[0]
    n_tok = bn * s_len
    alpha = (2 * depth) ** 0.25
    freq, sign = _rope_rows()
    pos3 = positions.reshape(bn, s_len, 1)
    row = lambda v: v.reshape(1, -1).astype(F32)
    for i in range(depth):
        lam_init = 0.8 - 0.6 * math.exp(-0.3 * i)
        wu, wv, wz, wq, wk, wvb, wzb, wga, wgb = [
            w_in[i, :, j * D_MODEL:(j + 1) * D_MODEL] for j in range(9)]
        x2 = x.reshape(n_tok, d)
        pa = _mixer_a(x2, wu.astype(BF16), wv.astype(BF16), wz.astype(BF16), wga.astype(BF16),
                      row(a_ln_g[i]), row(a_ln_b[i]), a_w_s[i], a_b_s[i].T,
                      w_branch_a[i].astype(BF16))
        q, k0, k1, vt, szb, sgb = _qkv(x, pos3, freq, sign, _head_interleave(wq).astype(BF16),
                                       _head_interleave(wk).astype(BF16), wvb.astype(BF16),
                                       wzb.astype(BF16), wgb.astype(BF16))
        o = _attention(q, k0, k1, vt, row(b_lam_q1[i]), row(b_lam_k1[i]), row(b_lam_q2[i]),
                       row(b_lam_k2[i]), b_subln_g[i].reshape(HEAD_DIM, 1).astype(F32), lam_init)
        y = _final(x2, p[i].reshape(n_tok, PLE_DIM), o.reshape(n_tok, d), szb.reshape(n_tok, d),
                   sgb.reshape(n_tok, d), pa, w_branch_b[i].astype(BF16), w_out[i].astype(BF16),
                   w_ple_gate[i].astype(BF16), w_ple[i].astype(BF16), row(ln_g[i]), row(ln_b[i]),
                   alpha)
        x = y.reshape(bn, s_len, d)
    return x
```

```python
import functools
import math

import jax
import jax.numpy as jnp
from jax import lax
from jax.experimental import pallas as pl
from jax.experimental.pallas import tpu as pltpu

F32 = jnp.float32
BF16 = jnp.bfloat16

D_MODEL = 1024
PLE_DIM = 256
CHUNK = 128
A_GROUPS = 8
HEADS = 8
QK_DIM = 64
HEAD_DIM = 2 * QK_DIM
ROPE_THETA = 10000.0
LN_EPS = 1e-5
RMS_EPS = 1e-5
NEG_BIG = -1e30

TOKEN_TILE = 512
SUB_TILE = 256
COL_TILE = 256
ATTN_TILE = 512
SCORE_COLS = 256
V_ROWS = HEAD_DIM + 16
FULL_UNROLL = 8
DIAG_UNROLL = 4
VMEM_LIMIT_BYTES = 48 * 1024 * 1024
ATTN_VMEM_LIMIT_BYTES = 56 * 1024 * 1024


def _dot(a, b):
    return jnp.dot(a, b, preferred_element_type=F32)


def _aligned(index, multiple):
    return index if isinstance(index, int) else pl.multiple_of(index, multiple)


def _gelu(x):
    return 0.5 * x * (1.0 + lax.erf(x * math.sqrt(0.5)))


def _sigmoid(x):
    return jax.nn.sigmoid(x)


def _layer_norm(x, g, b):
    mu = jnp.mean(x, axis=-1, keepdims=True)
    xc = x - mu
    var = jnp.mean(xc * xc, axis=-1, keepdims=True)
    return xc * lax.rsqrt(var + LN_EPS) * g + b


def _mixer_a_kernel(x_ref, wu_ref, wv_ref, wz_ref, wg_ref, lng_ref, lnb_ref, ws_ref, bst_ref,
                    wba_ref, out_ref, vb_ref, ya_ref):
    tm = x_ref.shape[0]
    row = lax.broadcasted_iota(jnp.int32, (CHUNK, CHUNK), 0)
    col = lax.broadcasted_iota(jnp.int32, (CHUNK, CHUNK), 1)
    causal = col <= row
    groups_per_block = COL_TILE // CHUNK
    for r0 in range(0, tm, SUB_TILE):
        sub = slice(r0, r0 + SUB_TILE)
        xb = x_ref[sub, :].astype(BF16)
        va = _gelu(_dot(xb, wv_ref[...]))
        vb_ref[sub, :] = _layer_norm(va, lng_ref[...], lnb_ref[...]).astype(BF16)
        for cb in range(D_MODEL // COL_TILE):
            cols = slice(cb * COL_TILE, (cb + 1) * COL_TILE)
            u = _gelu(_dot(xb, wu_ref[:, cols]))
            z = _dot(xb, wz_ref[:, cols])
            sz = z * _sigmoid(z)
            for gi in range(groups_per_block):
                g = cb * groups_per_block + gi
                gcols = slice(g * CHUNK, (g + 1) * CHUNK)
                lcols = slice(gi * CHUNK, (gi + 1) * CHUNK)
                ws_g = jnp.where(causal, ws_ref[g], 0.0).astype(BF16)
                bias = bst_ref[:, g:g + 1]
                n_chunks = SUB_TILE // CHUNK
                v_wide = jnp.concatenate(
                    [vb_ref[r0 + n * CHUNK:r0 + (n + 1) * CHUNK, gcols] for n in range(n_chunks)],
                    axis=1)
                sa_wide = _dot(ws_g, v_wide) + bias
                for n in range(n_chunks):
                    lrows = slice(n * CHUNK, (n + 1) * CHUNK)
                    sa = sa_wide[:, lrows]
                    ya_ref[r0 + n * CHUNK:r0 + (n + 1) * CHUNK, gcols] = (
                        u[lrows, lcols] * sa * sz[lrows, lcols]).astype(BF16)
        pa = _dot(ya_ref[sub, :], wba_ref[...])
        ga = _dot(xb, wg_ref[...])
        out_ref[sub, :] = (_sigmoid(ga) * pa).astype(out_ref.dtype)


def _mixer_a(x2, wu, wv, wz, wg, lng, lnb, ws, bst, wba):
    n_tok = x2.shape[0]
    tm = TOKEN_TILE
    full = lambda shape: pl.BlockSpec(shape, lambda i: (0,) * len(shape))
    return pl.pallas_call(
        _mixer_a_kernel,
        grid=(n_tok // tm,),
        in_specs=[
            pl.BlockSpec((tm, D_MODEL), lambda i: (i, 0)),
            full((D_MODEL, D_MODEL)), full((D_MODEL, D_MODEL)), full((D_MODEL, D_MODEL)),
            full((D_MODEL, D_MODEL)),
            full((1, D_MODEL)), full((1, D_MODEL)),
            full((A_GROUPS, CHUNK, CHUNK)), full((CHUNK, A_GROUPS)),
            full((D_MODEL, D_MODEL)),
        ],
        out_specs=pl.BlockSpec((tm, D_MODEL), lambda i: (i, 0)),
        out_shape=jax.ShapeDtypeStruct((n_tok, D_MODEL), BF16),
        scratch_shapes=[pltpu.VMEM((tm, D_MODEL), BF16), pltpu.VMEM((tm, D_MODEL), BF16)],
        compiler_params=pltpu.CompilerParams(
            dimension_semantics=("arbitrary",), vmem_limit_bytes=VMEM_LIMIT_BYTES),
        name="mixer_a",
    )(x2, wu, wv, wz, wg, lng, lnb, ws, bst, wba)


def _qkv_kernel(x_ref, pos_ref, freq_ref, sign_ref, wq_ref, wk_ref, wv_ref, wzb_ref, wgb_ref,
                qt_ref, k0_ref, k1_ref, vt_ref, szb_ref, sgb_ref, *, q_scale):
    tm = x_ref.shape[0]
    lane = lax.broadcasted_iota(jnp.int32, (SUB_TILE, HEAD_DIM), 1)
    in_map0 = (lane & (QK_DIM // 2)) == 0
    for h in range(HEADS):
        vt_ref[h, HEAD_DIM:V_ROWS, :] = jnp.ones((V_ROWS - HEAD_DIM, tm), BF16)

    for r0 in range(0, tm, SUB_TILE):
        sub = slice(r0, r0 + SUB_TILE)
        xb = x_ref[sub, :].astype(BF16)

        zb = _dot(xb, wzb_ref[...])
        szb_ref[sub, :] = (zb * _sigmoid(zb)).astype(BF16)
        gb = _dot(xb, wgb_ref[...])
        sgb_ref[sub, :] = _sigmoid(gb).astype(BF16)

        ang = pos_ref[sub, :].astype(F32) * freq_ref[...]
        cos = jnp.cos(ang)
        sin = jnp.sin(ang) * sign_ref[...]

        def rope(t):
            return t * cos + pltpu.roll(t, QK_DIM, 1) * sin

        v = _dot(xb, wv_ref[...])
        for h in range(HEADS):
            hs = slice(h * HEAD_DIM, (h + 1) * HEAD_DIM)
            vt_ref[h, 0:HEAD_DIM, sub] = v[:, hs].T.astype(BF16)
        q = _dot(xb, wq_ref[...])
        for h in range(HEADS):
            hs = slice(h * HEAD_DIM, (h + 1) * HEAD_DIM)
            qt_ref[h, :, sub] = (rope(q[:, hs]) * q_scale).T.astype(BF16)
        k = _dot(xb, wk_ref[...])
        for h in range(HEADS):
            hs = slice(h * HEAD_DIM, (h + 1) * HEAD_DIM)
            kr = rope(k[:, hs])
            k0_ref[h, sub, :] = jnp.where(in_map0, kr, 0.0).astype(BF16)
            k1_ref[h, sub, :] = jnp.where(in_map0, 0.0, kr).astype(BF16)


def _qkv(x3, pos3, freq, sign, wq, wk, wv, wzb, wgb):
    bn, s_len, _ = x3.shape
    tm = ATTN_TILE
    nkb = s_len // tm
    wspec = pl.BlockSpec((D_MODEL, D_MODEL), lambda b, i: (0, 0))
    lane_row = pl.BlockSpec((1, HEAD_DIM), lambda b, i: (0, 0))
    return pl.pallas_call(
        functools.partial(_qkv_kernel, q_scale=QK_DIM ** -0.5 * math.log2(math.e)),
        grid=(bn, nkb),
        in_specs=[
            pl.BlockSpec((None, tm, D_MODEL), lambda b, i: (b, i, 0)),
            pl.BlockSpec((None, tm, 1), lambda b, i: (b, i, 0)),
            lane_row, lane_row,
            wspec, wspec, wspec, wspec, wspec,
        ],
        out_specs=[
            pl.BlockSpec((None, HEADS, None, HEAD_DIM, tm), lambda b, i: (b, 0, i, 0, 0)),
            pl.BlockSpec((None, HEADS, tm, HEAD_DIM), lambda b, i: (b, 0, i, 0)),
            pl.BlockSpec((None, HEADS, tm, HEAD_DIM), lambda b, i: (b, 0, i, 0)),
            pl.BlockSpec((None, HEADS, None, V_ROWS, tm), lambda b, i: (b, 0, i, 0, 0)),
            pl.BlockSpec((None, tm, D_MODEL), lambda b, i: (b, i, 0)),
            pl.BlockSpec((None, tm, D_MODEL), lambda b, i: (b, i, 0)),
        ],
        out_shape=[
            jax.ShapeDtypeStruct((bn, HEADS, nkb, HEAD_DIM, tm), BF16),
            jax.ShapeDtypeStruct((bn, HEADS, s_len, HEAD_DIM), BF16),
            jax.ShapeDtypeStruct((bn, HEADS, s_len, HEAD_DIM), BF16),
            jax.ShapeDtypeStruct((bn, HEADS, nkb, V_ROWS, tm), BF16),
            jax.ShapeDtypeStruct((bn, s_len, D_MODEL), BF16),
            jax.ShapeDtypeStruct((bn, s_len, D_MODEL), BF16),
        ],
        compiler_params=pltpu.CompilerParams(
            dimension_semantics=("arbitrary", "arbitrary"), vmem_limit_bytes=VMEM_LIMIT_BYTES),
        name="qkv_proj",
    )(x3, pos3, freq, sign, wq, wk, wv, wzb, wgb)


def _attn_kernel(lq1_ref, lk1_ref, lq2_ref, lk2_ref, g_ref, qt_ref, k0_ref, k1_ref, vt_ref, o_ref,
                 sa_ref, sb_ref, bma_ref, bmb_ref, m_ref, acc_ref, *, lam_init):
    tq = ATTN_TILE
    tk = ATTN_TILE
    n_tiles = qt_ref.shape[0]
    n_streams = 2 * tq // SCORE_COLS
    per_map = n_streams // 2

    m_ref[...] = jnp.full(m_ref.shape, NEG_BIG, F32)
    acc_ref[...] = jnp.zeros_like(acc_ref)
    lam = (jnp.exp(jnp.sum(lq1_ref[...] * lk1_ref[...], keepdims=True))
           - jnp.exp(jnp.sum(lq2_ref[...] * lk2_ref[...], keepdims=True)) + lam_init)

    def scores(qi, kb, s_ref, bm_ref, on_diagonal):
        k_start = _aligned(kb * tk, tk)
        for c in range(n_streams):
            qoff = (c % per_map) * SCORE_COLS
            rows = min(tk, qoff + SCORE_COLS) if on_diagonal else tk
            k_ref = k0_ref if c < per_map else k1_ref
            s = _dot(k_ref[pl.ds(k_start, rows), :], qt_ref[qi, :, qoff:qoff + SCORE_COLS])
            s_ref[c, 0:rows, :] = s
            if not on_diagonal:
                bm_ref[c] = jnp.max(s, axis=0, keepdims=True)

    def softmax_pv(qi, kb, s_ref, bm_ref, on_diagonal):
        for c in range(n_streams):
            qoff = (c % per_map) * SCORE_COLS
            rows = min(tk, qoff + SCORE_COLS) if on_diagonal else tk
            st = s_ref[c, 0:rows, :]
            if on_diagonal:
                krow = lax.broadcasted_iota(jnp.int32, st.shape, 0)
                qcol = lax.broadcasted_iota(jnp.int32, st.shape, 1) + qoff
                st = jnp.where(krow <= qcol, st, NEG_BIG)
                bm = jnp.max(st, axis=0, keepdims=True)
            else:
                bm = bm_ref[c]
            m_old = m_ref[qi, c]
            m_new = jnp.maximum(m_old, bm)
            alpha = jnp.exp2(m_old - m_new)
            pt = jnp.exp2(st - m_new)
            m_ref[qi, c] = m_new
            acc_ref[qi, c] = alpha * acc_ref[qi, c] + _dot(vt_ref[kb, :, 0:rows], pt.astype(BF16))

    def finalize(qi):
        for j in range(per_map):
            a0 = acc_ref[qi, j]
            a1 = acc_ref[qi, per_map + j]
            ot = (a0[0:HEAD_DIM] / a0[HEAD_DIM:HEAD_DIM + 1]
                  - lam * (a1[0:HEAD_DIM] / a1[HEAD_DIM:HEAD_DIM + 1]))
            ms = jnp.mean(ot * ot, axis=0, keepdims=True)
            ot = ot * lax.rsqrt(ms + RMS_EPS) * g_ref[...] * (1.0 - lam_init)
            row0 = _aligned(qi * tq + j * SCORE_COLS, SCORE_COLS)
            o_ref[pl.ds(row0, SCORE_COLS), :] = ot.T.astype(o_ref.dtype)

    bufs = ((sa_ref, bma_ref), (sb_ref, bmb_ref))

    n_full = n_tiles * (n_tiles - 1) // 2
    assert n_full % FULL_UNROLL == 0 and FULL_UNROLL % 2 == 0
    assert n_tiles % DIAG_UNROLL == 0 and DIAG_UNROLL % 2 == 0

    def advance(qi, kb):
        wrap = kb + 1 >= qi
        return jnp.where(wrap, qi + 1, qi), jnp.where(wrap, 0, kb + 1)

    scores(1, 0, sa_ref, bma_ref, False)

    def full_body(_, carry):
        qi, kb = carry
        for step in range(FULL_UNROLL):
            cur, nxt = bufs[step % 2], bufs[(step + 1) % 2]
            nqi, nkb = advance(qi, kb)
            scores(jnp.minimum(nqi, n_tiles - 1), nkb, nxt[0], nxt[1], False)
            softmax_pv(qi, kb, cur[0], cur[1], False)
            qi, kb = nqi, nkb
        return qi, kb

    lax.fori_loop(0, n_full // FULL_UNROLL, full_body, (jnp.int32(1), jnp.int32(0)))

    scores(0, 0, sa_ref, bma_ref, True)

    def diag_body(u, carry):
        for step in range(DIAG_UNROLL):
            cur, nxt = bufs[step % 2], bufs[(step + 1) % 2]
            qi = DIAG_UNROLL * u + step
            nqi = jnp.minimum(qi + 1, n_tiles - 1)
            scores(nqi, nqi, nxt[0], nxt[1], True)
            softmax_pv(qi, qi, cur[0], cur[1], True)
            finalize(qi)
        return carry

    lax.fori_loop(0, n_tiles // DIAG_UNROLL, diag_body, 0)


def _attention(qt, k0, k1, vt, lq1, lk1, lq2, lk2, g_col, lam_init):
    bn, _, s_len, _ = k0.shape
    tq = ATTN_TILE
    n_tiles = s_len // tq
    n_streams = 2 * tq // SCORE_COLS
    vec = pl.BlockSpec((1, QK_DIM), lambda b, h: (0, 0))
    seq = pl.BlockSpec((None, None, s_len, HEAD_DIM), lambda b, h: (b, h, 0, 0))
    return pl.pallas_call(
        functools.partial(_attn_kernel, lam_init=lam_init),
        grid=(bn, HEADS),
        in_specs=[
            vec, vec, vec, vec,
            pl.BlockSpec((HEAD_DIM, 1), lambda b, h: (0, 0)),
            pl.BlockSpec((None, None, n_tiles, HEAD_DIM, tq), lambda b, h: (b, h, 0, 0, 0)),
            seq, seq,
            pl.BlockSpec((None, None, n_tiles, V_ROWS, tq), lambda b, h: (b, h, 0, 0, 0)),
        ],
        out_specs=pl.BlockSpec((None, s_len, HEAD_DIM), lambda b, h: (b, 0, h)),
        out_shape=jax.ShapeDtypeStruct((bn, s_len, HEADS * HEAD_DIM), BF16),
        scratch_shapes=[
            pltpu.VMEM((n_streams, tq, SCORE_COLS), F32),
            pltpu.VMEM((n_streams, tq, SCORE_COLS), F32),
            pltpu.VMEM((n_streams, 1, SCORE_COLS), F32),
            pltpu.VMEM((n_streams, 1, SCORE_COLS), F32),
            pltpu.VMEM((n_tiles, n_streams, 1, SCORE_COLS), F32),
            pltpu.VMEM((n_tiles, n_streams, V_ROWS, SCORE_COLS), F32),
        ],
        compiler_params=pltpu.CompilerParams(
            dimension_semantics=("arbitrary", "arbitrary"),
            vmem_limit_bytes=ATTN_VMEM_LIMIT_BYTES),
        name="diff_attn",
    )(lq1, lk1, lq2, lk2, g_col, qt, k0, k1, vt)


def _final_kernel(x_ref, p_ref, o_ref, szb_ref, sgb_ref, pa_ref, wbb_ref, wout_ref, wpg_ref,
                  wp_ref, lng_ref, lnb_ref, out_ref, *, alpha):
    for r0 in range(0, x_ref.shape[0], SUB_TILE):
        sub = slice(r0, r0 + SUB_TILE)
        x = x_ref[sub, :]
        xb = x.astype(BF16)
        yb = (o_ref[sub, :].astype(F32) * szb_ref[sub, :].astype(F32)).astype(BF16)
        merged = (pa_ref[sub, :].astype(F32)
                  + sgb_ref[sub, :].astype(F32) * _dot(yb, wbb_ref[...]))
        mix = _dot(merged.astype(BF16), wout_ref[...])
        ple = (_sigmoid(_dot(xb, wpg_ref[...]))
               * _dot(p_ref[sub, :].astype(BF16), wp_ref[...]))
        y = alpha * x + mix + ple
        out_ref[sub, :] = _layer_norm(y, lng_ref[...], lnb_ref[...]).astype(out_ref.dtype)


def _final(x2, p2, o2, szb2, sgb2, pa2, wbb, wout, wpg, wp, lng, lnb, alpha):
    n_tok = x2.shape[0]
    tm = TOKEN_TILE
    tile = lambda w: pl.BlockSpec((tm, w), lambda i: (i, 0))
    full = lambda shape: pl.BlockSpec(shape, lambda i: (0,) * len(shape))
    return pl.pallas_call(
        functools.partial(_final_kernel, alpha=alpha),
        grid=(n_tok // tm,),
        in_specs=[
            tile(D_MODEL), tile(PLE_DIM), tile(D_MODEL), tile(D_MODEL), tile(D_MODEL), tile(D_MODEL),
            full((D_MODEL, D_MODEL)), full((D_MODEL, D_MODEL)), full((D_MODEL, D_MODEL)),
            full((PLE_DIM, D_MODEL)), full((1, D_MODEL)), full((1, D_MODEL)),
        ],
        out_specs=tile(D_MODEL),
        out_shape=jax.ShapeDtypeStruct((n_tok, D_MODEL), x2.dtype),
        compiler_params=pltpu.CompilerParams(
            dimension_semantics=("arbitrary",), vmem_limit_bytes=VMEM_LIMIT_BYTES),
        name="merge_out",
    )(x2, p2, o2, szb2, sgb2, pa2, wbb, wout, wpg, wp, lng, lnb)


def _rope_rows():
    inv_freq = ROPE_THETA ** (-jnp.arange(0, QK_DIM, 2, dtype=F32) / QK_DIM)
    ones = jnp.ones_like(inv_freq)
    return (jnp.tile(inv_freq, 4).reshape(1, HEAD_DIM),
            jnp.concatenate([-ones, -ones, ones, ones]).reshape(1, HEAD_DIM))


def _head_interleave(w):
    d = w.shape[0]
    return w.reshape(d, HEADS, 2, 2, QK_DIM // 2).transpose(0, 1, 3, 2, 4).reshape(d, HEADS * HEAD_DIM)


def kernel(x, p, positions, w_in, a_ln_g, a_ln_b, a_w_s, a_b_s, b_lam_q1, b_lam_k1, b_lam_q2,
           b_lam_k2, b_subln_g, w_branch_a, w_branch_b, w_out, w_ple, w_ple_gate, ln_g, ln_b):
    bn, s_len, d = x.shape
    depth = w_in.shape[0]
    n_tok = bn * s_len
    alpha = (2 * depth) ** 0.25
    freq, sign = _rope_rows()
    pos3 = positions.reshape(bn, s_len, 1)
    row = lambda v: v.reshape(1, -1).astype(F32)
    for i in range(depth):
        lam_init = 0.8 - 0.6 * math.exp(-0.3 * i)
        wu, wv, wz, wq, wk, wvb, wzb, wga, wgb = [
            w_in[i, :, j * D_MODEL:(j + 1) * D_MODEL] for j in range(9)]
        x2 = x.reshape(n_tok, d)
        pa = _mixer_a(x2, wu.astype(BF16), wv.astype(BF16), wz.astype(BF16), wga.astype(BF16),
                      row(a_ln_g[i]), row(a_ln_b[i]), a_w_s[i], a_b_s[i].T,
                      w_branch_a[i].astype(BF16))
        qt, k0, k1, vt, szb, sgb = _qkv(x, pos3, freq, sign, _head_interleave(wq).astype(BF16),
                                       _head_interleave(wk).astype(BF16), wvb.astype(BF16),
                                       wzb.astype(BF16), wgb.astype(BF16))
        o = _attention(qt, k0, k1, vt, row(b_lam_q1[i]), row(b_lam_k1[i]), row(b_lam_q2[i]),
                       row(b_lam_k2[i]), b_subln_g[i].reshape(HEAD_DIM, 1).astype(F32), lam_init)
        y = _final(x2, p[i].reshape(n_tok, PLE_DIM), o.reshape(n_tok, d), szb.reshape(n_tok, d),
                   sgb.reshape(n_tok, d), pa, w_branch_b[i].astype(BF16), w_out[i].astype(BF16),
                   w_ple_gate[i].astype(BF16), w_ple[i].astype(BF16), row(ln_g[i]), row(ln_b[i]),
                   alpha)
        x = y.reshape(bn, s_len, d)
    return x
```

```python
import functools
import math

import jax
import jax.numpy as jnp
from jax import lax
from jax.experimental import pallas as pl
from jax.experimental.pallas import tpu as pltpu

F32 = jnp.float32
BF16 = jnp.bfloat16

D_MODEL = 1024
PLE_DIM = 256
CHUNK = 128
A_GROUPS = 8
HEADS = 8
QK_DIM = 64
HEAD_DIM = 2 * QK_DIM
ROPE_THETA = 10000.0
LN_EPS = 1e-5
RMS_EPS = 1e-5
NEG_BIG = -1e30
(IN_COL_UA, IN_COL_VA, IN_COL_ZA, IN_COL_Q, IN_COL_K, IN_COL_V, IN_COL_ZB, IN_COL_GA,
 IN_COL_GB) = range(9)

TOKEN_TILE = 1024
SUB_TILE = 256
COL_TILE = 256
ATTN_TILE = 512
SCORE_COLS = 256
V_ROWS = HEAD_DIM + 16
FULL_UNROLL = 12
DIAG_UNROLL = 4
VMEM_LIMIT_BYTES = 48 * 1024 * 1024
ATTN_VMEM_LIMIT_BYTES = 56 * 1024 * 1024


def _dot(a, b):
    return jnp.dot(a, b, preferred_element_type=F32)


def _resident(shape):
    return pl.BlockSpec(shape, lambda *_: (0,) * len(shape), pipeline_mode=pl.Buffered(1))


def _resident_cols(j):
    return pl.BlockSpec((D_MODEL, D_MODEL), lambda *_: (0, j), pipeline_mode=pl.Buffered(1))


def _aligned(index, multiple):
    return index if isinstance(index, int) else pl.multiple_of(index, multiple)


def _gelu(x):
    return 0.5 * x * (1.0 + lax.erf(x * math.sqrt(0.5)))


def _sigmoid(x):
    return jax.nn.sigmoid(x)


def _layer_norm(x, g, b):
    mu = jnp.mean(x, axis=-1, keepdims=True)
    xc = x - mu
    var = jnp.mean(xc * xc, axis=-1, keepdims=True)
    return xc * lax.rsqrt(var + LN_EPS) * g + b


def _mixer_a_kernel(x_ref, wu_ref, wv_ref, wz_ref, wg_ref, lng_ref, lnb_ref, ws_ref, bst_ref,
                    wba_ref, out_ref, vb_ref, ya_ref):
    tm = x_ref.shape[0]
    row = lax.broadcasted_iota(jnp.int32, (CHUNK, CHUNK), 0)
    col = lax.broadcasted_iota(jnp.int32, (CHUNK, CHUNK), 1)
    causal = col <= row
    groups_per_block = COL_TILE // CHUNK
    for r0 in range(0, tm, SUB_TILE):
        sub = slice(r0, r0 + SUB_TILE)
        xb = x_ref[sub, :].astype(BF16)
        va = _gelu(_dot(xb, wv_ref[...]))
        vb_ref[sub, :] = _layer_norm(va, lng_ref[...], lnb_ref[...]).astype(BF16)
        for cb in range(D_MODEL // COL_TILE):
            cols = slice(cb * COL_TILE, (cb + 1) * COL_TILE)
            u = _gelu(_dot(xb, wu_ref[:, cols]))
            z = _dot(xb, wz_ref[:, cols])
            sz = z * _sigmoid(z)
            for gi in range(groups_per_block):
                g = cb * groups_per_block + gi
                gcols = slice(g * CHUNK, (g + 1) * CHUNK)
                lcols = slice(gi * CHUNK, (gi + 1) * CHUNK)
                ws_g = jnp.where(causal, ws_ref[g], 0.0).astype(BF16)
                bias = bst_ref[:, g:g + 1]
                n_chunks = SUB_TILE // CHUNK
                v_wide = jnp.concatenate(
                    [vb_ref[r0 + n * CHUNK:r0 + (n + 1) * CHUNK, gcols] for n in range(n_chunks)],
                    axis=1)
                sa_wide = _dot(ws_g, v_wide) + bias
                for n in range(n_chunks):
                    lrows = slice(n * CHUNK, (n + 1) * CHUNK)
                    sa = sa_wide[:, lrows]
                    ya_ref[r0 + n * CHUNK:r0 + (n + 1) * CHUNK, gcols] = (
                        u[lrows, lcols] * sa * sz[lrows, lcols]).astype(BF16)
        pa = _dot(ya_ref[sub, :], wba_ref[...])
        ga = _dot(xb, wg_ref[...])
        out_ref[sub, :] = (_sigmoid(ga) * pa).astype(out_ref.dtype)


def _mixer_a(x2, w_all, lng, lnb, ws, bst, wba):
    n_tok = x2.shape[0]
    tm = TOKEN_TILE
    full = _resident
    return pl.pallas_call(
        _mixer_a_kernel,
        grid=(n_tok // tm,),
        in_specs=[
            pl.BlockSpec((tm, D_MODEL), lambda i: (i, 0)),
            _resident_cols(IN_COL_UA), _resident_cols(IN_COL_VA), _resident_cols(IN_COL_ZA),
            _resident_cols(IN_COL_GA),
            full((1, D_MODEL)), full((1, D_MODEL)),
            full((A_GROUPS, CHUNK, CHUNK)), full((CHUNK, A_GROUPS)),
            full((D_MODEL, D_MODEL)),
        ],
        out_specs=pl.BlockSpec((tm, D_MODEL), lambda i: (i, 0)),
        out_shape=jax.ShapeDtypeStruct((n_tok, D_MODEL), BF16),
        scratch_shapes=[pltpu.VMEM((tm, D_MODEL), BF16), pltpu.VMEM((tm, D_MODEL), BF16)],
        compiler_params=pltpu.CompilerParams(
            dimension_semantics=("arbitrary",), vmem_limit_bytes=VMEM_LIMIT_BYTES),
        name="mixer_a",
    )(x2, w_all, w_all, w_all, w_all, lng, lnb, ws, bst, wba)


def _qkv_kernel(x_ref, pos_ref, freq_ref, sign_ref, wq_ref, wk_ref, wv_ref, wzb_ref, wgb_ref,
                qt_ref, k0_ref, k1_ref, vt_ref, szb_ref, sgb_ref, *, q_scale):
    tm = x_ref.shape[0]
    lane = lax.broadcasted_iota(jnp.int32, (SUB_TILE, HEAD_DIM), 1)
    in_map0 = (lane & QK_DIM) == 0
    in_lo = (lane & (QK_DIM // 2)) == 0
    ta = vt_ref.shape[-1]
    for h in range(HEADS):
        for tile in range(tm // ta):
            vt_ref[h, tile, HEAD_DIM:V_ROWS, :] = jnp.ones((V_ROWS - HEAD_DIM, ta), BF16)

    for r0 in range(0, tm, SUB_TILE):
        sub = slice(r0, r0 + SUB_TILE)
        tile, tcols = r0 // ta, slice(r0 % ta, r0 % ta + SUB_TILE)
        xb = x_ref[sub, :].astype(BF16)

        zb = _dot(xb, wzb_ref[...])
        szb_ref[sub, :] = (zb * _sigmoid(zb)).astype(BF16)
        gb = _dot(xb, wgb_ref[...])
        sgb_ref[sub, :] = _sigmoid(gb).astype(BF16)

        ang = pos_ref[sub, :].astype(F32) * freq_ref[...]
        cos = jnp.cos(ang)
        sin = jnp.sin(ang) * sign_ref[...]

        def rope(t):
            up = pltpu.roll(t, HEAD_DIM - QK_DIM // 2, 1)
            down = pltpu.roll(t, QK_DIM // 2, 1)
            return t * cos + jnp.where(in_lo, up, down) * sin

        v = _dot(xb, wv_ref[...])
        for h in range(HEADS):
            hs = slice(h * HEAD_DIM, (h + 1) * HEAD_DIM)
            vt_ref[h, tile, 0:HEAD_DIM, tcols] = v[:, hs].T.astype(BF16)
        q = _dot(xb, wq_ref[...])
        for h in range(HEADS):
            hs = slice(h * HEAD_DIM, (h + 1) * HEAD_DIM)
            qt_ref[h, tile, :, tcols] = (rope(q[:, hs]) * q_scale).T.astype(BF16)
        k = _dot(xb, wk_ref[...])
        for h in range(HEADS):
            hs = slice(h * HEAD_DIM, (h + 1) * HEAD_DIM)
            kr = rope(k[:, hs])
            k0_ref[h, sub, :] = jnp.where(in_map0, kr, 0.0).astype(BF16)
            k1_ref[h, sub, :] = jnp.where(in_map0, 0.0, kr).astype(BF16)


def _qkv(x3, pos3, freq, sign, w_all):
    bn, s_len, _ = x3.shape
    tm = TOKEN_TILE
    ta = ATTN_TILE
    per_step = tm // ta
    lane_row = _resident((1, HEAD_DIM))
    return pl.pallas_call(
        functools.partial(_qkv_kernel, q_scale=QK_DIM ** -0.5 * math.log2(math.e)),
        grid=(bn, s_len // tm),
        in_specs=[
            pl.BlockSpec((None, tm, D_MODEL), lambda b, i: (b, i, 0)),
            pl.BlockSpec((None, tm, 1), lambda b, i: (b, i, 0)),
            lane_row, lane_row,
            _resident_cols(IN_COL_Q), _resident_cols(IN_COL_K), _resident_cols(IN_COL_V),
            _resident_cols(IN_COL_ZB), _resident_cols(IN_COL_GB),
        ],
        out_specs=[
            pl.BlockSpec((None, HEADS, per_step, HEAD_DIM, ta), lambda b, i: (b, 0, i, 0, 0)),
            pl.BlockSpec((None, HEADS, tm, HEAD_DIM), lambda b, i: (b, 0, i, 0)),
            pl.BlockSpec((None, HEADS, tm, HEAD_DIM), lambda b, i: (b, 0, i, 0)),
            pl.BlockSpec((None, HEADS, per_step, V_ROWS, ta), lambda b, i: (b, 0, i, 0, 0)),
            pl.BlockSpec((None, tm, D_MODEL), lambda b, i: (b, i, 0)),
            pl.BlockSpec((None, tm, D_MODEL), lambda b, i: (b, i, 0)),
        ],
        out_shape=[
            jax.ShapeDtypeStruct((bn, HEADS, s_len // ta, HEAD_DIM, ta), BF16),
            jax.ShapeDtypeStruct((bn, HEADS, s_len, HEAD_DIM), BF16),
            jax.ShapeDtypeStruct((bn, HEADS, s_len, HEAD_DIM), BF16),
            jax.ShapeDtypeStruct((bn, HEADS, s_len // ta, V_ROWS, ta), BF16),
            jax.ShapeDtypeStruct((bn, s_len, D_MODEL), BF16),
            jax.ShapeDtypeStruct((bn, s_len, D_MODEL), BF16),
        ],
        compiler_params=pltpu.CompilerParams(
            dimension_semantics=("arbitrary", "arbitrary"), vmem_limit_bytes=VMEM_LIMIT_BYTES),
        name="qkv_proj",
    )(x3, pos3, freq, sign, w_all, w_all, w_all, w_all, w_all)


def _attn_kernel(lq1_ref, lk1_ref, lq2_ref, lk2_ref, g_ref, qt_ref, k0_ref, k1_ref, vt_ref, o_ref,
                 sa_ref, sb_ref, bma_ref, bmb_ref, m_ref, acc_ref, *, lam_init):
    tq = ATTN_TILE
    tk = ATTN_TILE
    n_tiles = qt_ref.shape[0]
    n_streams = 2 * tq // SCORE_COLS
    per_map = n_streams // 2

    m_ref[...] = jnp.full(m_ref.shape, NEG_BIG, F32)
    acc_ref[...] = jnp.zeros_like(acc_ref)
    lam = (jnp.exp(jnp.sum(lq1_ref[...] * lk1_ref[...], keepdims=True))
           - jnp.exp(jnp.sum(lq2_ref[...] * lk2_ref[...], keepdims=True)) + lam_init)

    def scores(qi, kb, s_ref, bm_ref, on_diagonal):
        k_start = _aligned(kb * tk, tk)
        for c in range(n_streams):
            qoff = (c % per_map) * SCORE_COLS
            rows = min(tk, qoff + SCORE_COLS) if on_diagonal else tk
            k_ref = k0_ref if c < per_map else k1_ref
            s = _dot(k_ref[pl.ds(k_start, rows), :], qt_ref[qi, :, qoff:qoff + SCORE_COLS])
            s_ref[c, 0:rows, :] = s
            if not on_diagonal:
                bm_ref[c] = jnp.max(s, axis=0, keepdims=True)

    def softmax_pv(qi, kb, s_ref, bm_ref, on_diagonal):
        for c in range(n_streams):
            qoff = (c % per_map) * SCORE_COLS
            rows = min(tk, qoff + SCORE_COLS) if on_diagonal else tk
            st = s_ref[c, 0:rows, :]
            if on_diagonal:
                krow = lax.broadcasted_iota(jnp.int32, st.shape, 0)
                qcol = lax.broadcasted_iota(jnp.int32, st.shape, 1) + qoff
                st = jnp.where(krow <= qcol, st, NEG_BIG)
                bm = jnp.max(st, axis=0, keepdims=True)
            else:
                bm = bm_ref[c]
            m_old = m_ref[qi, c]
            m_new = jnp.maximum(m_old, bm)
            alpha = jnp.exp2(m_old - m_new)
            pt = jnp.exp2(st - m_new)
            m_ref[qi, c] = m_new
            acc_ref[qi, c] = alpha * acc_ref[qi, c] + _dot(vt_ref[kb, :, 0:rows], pt.astype(BF16))

    def finalize(qi):
        for j in range(per_map):
            a0 = acc_ref[qi, j]
            a1 = acc_ref[qi, per_map + j]
            ot = (a0[0:HEAD_DIM] / a0[HEAD_DIM:HEAD_DIM + 1]
                  - lam * (a1[0:HEAD_DIM] / a1[HEAD_DIM:HEAD_DIM + 1]))
            ms = jnp.mean(ot * ot, axis=0, keepdims=True)
            ot = ot * lax.rsqrt(ms + RMS_EPS) * g_ref[...] * (1.0 - lam_init)
            row0 = _aligned(qi * tq + j * SCORE_COLS, SCORE_COLS)
            o_ref[pl.ds(row0, SCORE_COLS), :] = ot.T.astype(o_ref.dtype)

    bufs = ((sa_ref, bma_ref), (sb_ref, bmb_ref))

    n_full = n_tiles * (n_tiles - 1) // 2
    assert n_full % FULL_UNROLL == 0 and FULL_UNROLL % 2 == 0
    assert n_tiles % DIAG_UNROLL == 0 and DIAG_UNROLL % 2 == 0

    def advance(qi, kb):
        wrap = kb + 1 >= qi
        return jnp.where(wrap, qi + 1, qi), jnp.where(wrap, 0, kb + 1)

    scores(1, 0, sa_ref, bma_ref, False)

    def full_body(_, carry):
        qi, kb = carry
        for step in range(FULL_UNROLL):
            cur, nxt = bufs[step % 2], bufs[(step + 1) % 2]
            nqi, nkb = advance(qi, kb)
            scores(jnp.minimum(nqi, n_tiles - 1), nkb, nxt[0], nxt[1], False)
            softmax_pv(qi, kb, cur[0], cur[1], False)
            qi, kb = nqi, nkb
        return qi, kb

    lax.fori_loop(0, n_full // FULL_UNROLL, full_body, (jnp.int32(1), jnp.int32(0)))

    scores(0, 0, sa_ref, bma_ref, True)

    def diag_body(u, carry):
        for step in range(DIAG_UNROLL):
            cur, nxt = bufs[step % 2], bufs[(step + 1) % 2]
            qi = DIAG_UNROLL * u + step
            nqi = jnp.minimum(qi + 1, n_tiles - 1)
            scores(nqi, nqi, nxt[0], nxt[1], True)
            softmax_pv(qi, qi, cur[0], cur[1], True)
            finalize(qi)
        return carry

    lax.fori_loop(0, n_tiles // DIAG_UNROLL, diag_body, 0)


def _attention(qt, k0, k1, vt, lq1, lk1, lq2, lk2, g_col, lam_init):
    bn, _, s_len, _ = k0.shape
    tq = ATTN_TILE
    n_tiles = s_len // tq
    n_streams = 2 * tq // SCORE_COLS
    vec = pl.BlockSpec((1, QK_DIM), lambda b, h: (0, 0))
    seq = pl.BlockSpec((None, None, s_len, HEAD_DIM), lambda b, h: (b, h, 0, 0))
    return pl.pallas_call(
        functools.partial(_attn_kernel, lam_init=lam_init),
        grid=(bn, HEADS),
        in_specs=[
            vec, vec, vec, vec,
            pl.BlockSpec((HEAD_DIM, 1), lambda b, h: (0, 0)),
            pl.BlockSpec((None, None, n_tiles, HEAD_DIM, tq), lambda b, h: (b, h, 0, 0, 0)),
            seq, seq,
            pl.BlockSpec((None, None, n_tiles, V_ROWS, tq), lambda b, h: (b, h, 0, 0, 0)),
        ],
        out_specs=pl.BlockSpec((None, s_len, HEAD_DIM), lambda b, h: (b, 0, h)),
        out_shape=jax.ShapeDtypeStruct((bn, s_len, HEADS * HEAD_DIM), BF16),
        scratch_shapes=[
            pltpu.VMEM((n_streams, tq, SCORE_COLS), F32),
            pltpu.VMEM((n_streams, tq, SCORE_COLS), F32),
            pltpu.VMEM((n_streams, 1, SCORE_COLS), F32),
            pltpu.VMEM((n_streams, 1, SCORE_COLS), F32),
            pltpu.VMEM((n_tiles, n_streams, 1, SCORE_COLS), F32),
            pltpu.VMEM((n_tiles, n_streams, V_ROWS, SCORE_COLS), F32),
        ],
        compiler_params=pltpu.CompilerParams(
            dimension_semantics=("arbitrary", "arbitrary"),
            vmem_limit_bytes=ATTN_VMEM_LIMIT_BYTES),
        name="diff_attn",
    )(lq1, lk1, lq2, lk2, g_col, qt, k0, k1, vt)


def _final_kernel(x_ref, p_ref, o_ref, szb_ref, sgb_ref, pa_ref, wbb_ref, wout_ref, wpg_ref,
                  wp_ref, lng_ref, lnb_ref, out_ref, *, alpha):
    for r0 in range(0, x_ref.shape[0], SUB_TILE):
        sub = slice(r0, r0 + SUB_TILE)
        x = x_ref[sub, :]
        xb = x.astype(BF16)
        yb = (o_ref[sub, :].astype(F32) * szb_ref[sub, :].astype(F32)).astype(BF16)
        merged = (pa_ref[sub, :].astype(F32)
                  + sgb_ref[sub, :].astype(F32) * _dot(yb, wbb_ref[...]))
        mix = _dot(merged.astype(BF16), wout_ref[...])
        ple = (_sigmoid(_dot(xb, wpg_ref[...]))
               * _dot(p_ref[sub, :].astype(BF16), wp_ref[...]))
        y = alpha * x + mix + ple
        out_ref[sub, :] = _layer_norm(y, lng_ref[...], lnb_ref[...]).astype(out_ref.dtype)


def _final(x2, p2, o2, szb2, sgb2, pa2, wbb, wout, wpg, wp, lng, lnb, alpha):
    n_tok = x2.shape[0]
    tm = TOKEN_TILE
    tile = lambda w: pl.BlockSpec((tm, w), lambda i: (i, 0))
    full = _resident
    return pl.pallas_call(
        functools.partial(_final_kernel, alpha=alpha),
        grid=(n_tok // tm,),
        in_specs=[
            tile(D_MODEL), tile(PLE_DIM), tile(D_MODEL), tile(D_MODEL), tile(D_MODEL), tile(D_MODEL),
            full((D_MODEL, D_MODEL)), full((D_MODEL, D_MODEL)), full((D_MODEL, D_MODEL)),
            full((PLE_DIM, D_MODEL)), full((1, D_MODEL)), full((1, D_MODEL)),
        ],
        out_specs=tile(D_MODEL),
        out_shape=jax.ShapeDtypeStruct((n_tok, D_MODEL), x2.dtype),
        compiler_params=pltpu.CompilerParams(
            dimension_semantics=("arbitrary",), vmem_limit_bytes=VMEM_LIMIT_BYTES),
        name="merge_out",
    )(x2, p2, o2, szb2, sgb2, pa2, wbb, wout, wpg, wp, lng, lnb)


def _rope_rows():
    inv_freq = ROPE_THETA ** (-jnp.arange(0, QK_DIM, 2, dtype=F32) / QK_DIM)
    ones = jnp.ones_like(inv_freq)
    return (jnp.tile(inv_freq, 4).reshape(1, HEAD_DIM),
            jnp.concatenate([-ones, ones, -ones, ones]).reshape(1, HEAD_DIM))


def kernel(x, p, positions, w_in, a_ln_g, a_ln_b, a_w_s, a_b_s, b_lam_q1, b_lam_k1, b_lam_q2,
           b_lam_k2, b_subln_g, w_branch_a, w_branch_b, w_out, w_ple, w_ple_gate, ln_g, ln_b):
    bn, s_len, d = x.shape
    depth = w_in.shape[0]
    n_tok = bn * s_len
    alpha = (2 * depth) ** 0.25
    freq, sign = _rope_rows()
    pos3 = positions.reshape(bn, s_len, 1)
    row = lambda v: v.reshape(1, -1).astype(F32)
    for i in range(depth):
        lam_init = 0.8 - 0.6 * math.exp(-0.3 * i)
        w_all = w_in[i].astype(BF16)
        x2 = x.reshape(n_tok, d)
        pa = _mixer_a(x2, w_all, row(a_ln_g[i]), row(a_ln_b[i]), a_w_s[i], a_b_s[i].T,
                      w_branch_a[i].astype(BF16))
        qt, k0, k1, vt, szb, sgb = _qkv(x, pos3, freq, sign, w_all)
        o = _attention(qt, k0, k1, vt, row(b_lam_q1[i]), row(b_lam_k1[i]), row(b_lam_q2[i]),
                       row(b_lam_k2[i]), b_subln_g[i].reshape(HEAD_DIM, 1).astype(F32), lam_init)
        y = _final(x2, p[i].reshape(n_tok, PLE_DIM), o.reshape(n_tok, d), szb.reshape(n_tok, d),
                   sgb.reshape(n_tok, d), pa, w_branch_b[i].astype(BF16), w_out[i].astype(BF16),
                   w_ple_gate[i].astype(BF16), w_ple[i].astype(BF16), row(ln_g[i]), row(ln_b[i]),
                   alpha)
        x = y.reshape(bn, s_len, d)
    return x
```

```python
import functools
import math

import jax
import jax.numpy as jnp
from jax import lax
from jax.experimental import pallas as pl
from jax.experimental.pallas import tpu as pltpu

F32 = jnp.float32
BF16 = jnp.bfloat16

D_MODEL = 1024
PLE_DIM = 256
CHUNK = 128
A_GROUPS = 8
HEADS = 8
QK_DIM = 64
HEAD_DIM = 2 * QK_DIM
ROPE_THETA = 10000.0
LN_EPS = 1e-5
RMS_EPS = 1e-5
NEG_BIG = -1e30
(IN_COL_UA, IN_COL_VA, IN_COL_ZA, IN_COL_Q, IN_COL_K, IN_COL_V, IN_COL_ZB, IN_COL_GA,
 IN_COL_GB) = range(9)

TOKEN_TILE = 1024
SUB_TILE = 256
COL_TILE = 256
ATTN_TILE = 512
SCORE_COLS = 256
V_ROWS = HEAD_DIM + 16
FULL_UNROLL = 12
DIAG_UNROLL = 4
VMEM_LIMIT_BYTES = 48 * 1024 * 1024
ATTN_VMEM_LIMIT_BYTES = 56 * 1024 * 1024


def _dot(a, b):
    return jnp.dot(a, b, preferred_element_type=F32)


def _resident(shape):
    return pl.BlockSpec(shape, lambda *_: (0,) * len(shape), pipeline_mode=pl.Buffered(1))


def _resident_cols(j):
    return pl.BlockSpec((D_MODEL, D_MODEL), lambda *_: (0, j), pipeline_mode=pl.Buffered(1))


def _aligned(index, multiple):
    return index if isinstance(index, int) else pl.multiple_of(index, multiple)


def _gelu(x):
    return 0.5 * x * (1.0 + lax.erf(x * math.sqrt(0.5)))


def _sigmoid(x):
    return jax.nn.sigmoid(x)


def _layer_norm(x, g, b):
    mu = jnp.mean(x, axis=-1, keepdims=True)
    xc = x - mu
    var = jnp.mean(xc * xc, axis=-1, keepdims=True)
    return xc * lax.rsqrt(var + LN_EPS) * g + b


def _mixer_a_kernel(x_ref, wu_ref, wv_ref, wz_ref, wg_ref, lng_ref, lnb_ref, ws_ref, bst_ref,
                    wba_ref, out_ref, vb_ref, ya_ref):
    tm = x_ref.shape[0]
    row = lax.broadcasted_iota(jnp.int32, (CHUNK, CHUNK), 0)
    col = lax.broadcasted_iota(jnp.int32, (CHUNK, CHUNK), 1)
    causal = col <= row
    groups_per_block = COL_TILE // CHUNK
    for r0 in range(0, tm, SUB_TILE):
        sub = slice(r0, r0 + SUB_TILE)
        xb = x_ref[sub, :].astype(BF16)
        va = _gelu(_dot(xb, wv_ref[...]))
        ga = _dot(xb, wg_ref[...])
        vb_ref[sub, :] = _layer_norm(va, lng_ref[...], lnb_ref[...]).astype(BF16)
        for cb in range(D_MODEL // COL_TILE):
            cols = slice(cb * COL_TILE, (cb + 1) * COL_TILE)
            u = _gelu(_dot(xb, wu_ref[:, cols]))
            z = _dot(xb, wz_ref[:, cols])
            sz = z * _sigmoid(z)
            for gi in range(groups_per_block):
                g = cb * groups_per_block + gi
                gcols = slice(g * CHUNK, (g + 1) * CHUNK)
                lcols = slice(gi * CHUNK, (gi + 1) * CHUNK)
                ws_g = jnp.where(causal, ws_ref[g], 0.0).astype(BF16)
                bias = bst_ref[:, g:g + 1]
                n_chunks = SUB_TILE // CHUNK
                v_wide = jnp.concatenate(
                    [vb_ref[r0 + n * CHUNK:r0 + (n + 1) * CHUNK, gcols] for n in range(n_chunks)],
                    axis=1)
                sa_wide = _dot(ws_g, v_wide) + bias
                for n in range(n_chunks):
                    lrows = slice(n * CHUNK, (n + 1) * CHUNK)
                    sa = sa_wide[:, lrows]
                    ya_ref[r0 + n * CHUNK:r0 + (n + 1) * CHUNK, gcols] = (
                        u[lrows, lcols] * sa * sz[lrows, lcols]).astype(BF16)
        pa = _dot(ya_ref[sub, :], wba_ref[...])
        out_ref[sub, :] = (_sigmoid(ga) * pa).astype(out_ref.dtype)


def _mixer_a(x2, w_all, lng, lnb, ws, bst, wba):
    n_tok = x2.shape[0]
    tm = TOKEN_TILE
    full = _resident
    return pl.pallas_call(
        _mixer_a_kernel,
        grid=(n_tok // tm,),
        in_specs=[
            pl.BlockSpec((tm, D_MODEL), lambda i: (i, 0)),
            _resident_cols(IN_COL_UA), _resident_cols(IN_COL_VA), _resident_cols(IN_COL_ZA),
            _resident_cols(IN_COL_GA),
            full((1, D_MODEL)), full((1, D_MODEL)),
            full((A_GROUPS, CHUNK, CHUNK)), full((CHUNK, A_GROUPS)),
            full((D_MODEL, D_MODEL)),
        ],
        out_specs=pl.BlockSpec((tm, D_MODEL), lambda i: (i, 0)),
        out_shape=jax.ShapeDtypeStruct((n_tok, D_MODEL), BF16),
        scratch_shapes=[pltpu.VMEM((tm, D_MODEL), BF16), pltpu.VMEM((tm, D_MODEL), BF16)],
        compiler_params=pltpu.CompilerParams(
            dimension_semantics=("arbitrary",), vmem_limit_bytes=VMEM_LIMIT_BYTES),
        name="mixer_a",
    )(x2, w_all, w_all, w_all, w_all, lng, lnb, ws, bst, wba)


def _qkv_kernel(x_ref, pos_ref, freq_ref, sign_ref, wq_ref, wk_ref, wv_ref, wzb_ref, wgb_ref,
                qt_ref, k0_ref, k1_ref, vt_ref, szb_ref, sgb_ref, *, q_scale):
    tm = x_ref.shape[0]
    lane = lax.broadcasted_iota(jnp.int32, (SUB_TILE, HEAD_DIM), 1)
    in_map0 = (lane & QK_DIM) == 0
    in_lo = (lane & (QK_DIM // 2)) == 0
    ta = vt_ref.shape[-1]
    for h in range(HEADS):
        for tile in range(tm // ta):
            vt_ref[h, tile, HEAD_DIM:V_ROWS, :] = jnp.ones((V_ROWS - HEAD_DIM, ta), BF16)

    for r0 in range(0, tm, SUB_TILE):
        sub = slice(r0, r0 + SUB_TILE)
        tile, tcols = r0 // ta, slice(r0 % ta, r0 % ta + SUB_TILE)
        xb = x_ref[sub, :].astype(BF16)

        v = _dot(xb, wv_ref[...])
        for h in range(HEADS):
            hs = slice(h * HEAD_DIM, (h + 1) * HEAD_DIM)
            vt_ref[h, tile, 0:HEAD_DIM, tcols] = v[:, hs].T.astype(BF16)

        ang =pos_ref[sub, :].astype(F32) * freq_ref[...]
        cos = jnp.cos(ang)
        sin = jnp.sin(ang) * sign_ref[...]

        def rope(t):
            up = pltpu.roll(t, HEAD_DIM - QK_DIM // 2, 1)
            down = pltpu.roll(t, QK_DIM // 2, 1)
            return t * cos + jnp.where(in_lo, up, down) * sin

        q = _dot(xb, wq_ref[...])
        for h in range(HEADS):
            hs = slice(h * HEAD_DIM, (h + 1) * HEAD_DIM)
            qt_ref[h, tile, :, tcols] = (rope(q[:, hs]) * q_scale).T.astype(BF16)
        k = _dot(xb, wk_ref[...])
        for h in range(HEADS):
            hs = slice(h * HEAD_DIM, (h + 1) * HEAD_DIM)
            kr = rope(k[:, hs])
            k0_ref[h, sub, :] = jnp.where(in_map0, kr, 0.0).astype(BF16)
            k1_ref[h, sub, :] = jnp.where(in_map0, 0.0, kr).astype(BF16)

        zb = _dot(xb, wzb_ref[...])
        szb_ref[sub, :] = (zb * _sigmoid(zb)).astype(BF16)
        gb = _dot(xb, wgb_ref[...])
        sgb_ref[sub, :] = _sigmoid(gb).astype(BF16)


def _qkv(x3, pos3, freq, sign, w_all):
    bn, s_len, _ = x3.shape
    tm = TOKEN_TILE
    ta = ATTN_TILE
    per_step = tm // ta
    lane_row = _resident((1, HEAD_DIM))
    return pl.pallas_call(
        functools.partial(_qkv_kernel, q_scale=QK_DIM ** -0.5 * math.log2(math.e)),
        grid=(bn, s_len // tm),
        in_specs=[
            pl.BlockSpec((None, tm, D_MODEL), lambda b, i: (b, i, 0)),
            pl.BlockSpec((None, tm, 1), lambda b, i: (b, i, 0)),
            lane_row, lane_row,
            _resident_cols(IN_COL_Q), _resident_cols(IN_COL_K), _resident_cols(IN_COL_V),
            _resident_cols(IN_COL_ZB), _resident_cols(IN_COL_GB),
        ],
        out_specs=[
            pl.BlockSpec((None, HEADS, per_step, HEAD_DIM, ta), lambda b, i: (b, 0, i, 0, 0)),
            pl.BlockSpec((None, HEADS, tm, HEAD_DIM), lambda b, i: (b, 0, i, 0)),
            pl.BlockSpec((None, HEADS, tm, HEAD_DIM), lambda b, i: (b, 0, i, 0)),
            pl.BlockSpec((None, HEADS, per_step, V_ROWS, ta), lambda b, i: (b, 0, i, 0, 0)),
            pl.BlockSpec((None, tm, D_MODEL), lambda b, i: (b, i, 0)),
            pl.BlockSpec((None, tm, D_MODEL), lambda b, i: (b, i, 0)),
        ],
        out_shape=[
            jax.ShapeDtypeStruct((bn, HEADS, s_len // ta, HEAD_DIM, ta), BF16),
            jax.ShapeDtypeStruct((bn, HEADS, s_len, HEAD_DIM), BF16),
            jax.ShapeDtypeStruct((bn, HEADS, s_len, HEAD_DIM), BF16),
            jax.ShapeDtypeStruct((bn, HEADS, s_len // ta, V_ROWS, ta), BF16),
            jax.ShapeDtypeStruct((bn, s_len, D_MODEL), BF16),
            jax.ShapeDtypeStruct((bn, s_len, D_MODEL), BF16),
        ],
        compiler_params=pltpu.CompilerParams(
            dimension_semantics=("arbitrary", "arbitrary"), vmem_limit_bytes=VMEM_LIMIT_BYTES),
        name="qkv_proj",
    )(x3, pos3, freq, sign, w_all, w_all, w_all, w_all, w_all)


def _attn_kernel(lq1_ref, lk1_ref, lq2_ref, lk2_ref, g_ref, qt_ref, k0_ref, k1_ref, vt_ref, o_ref,
                 sa_ref, sb_ref, bma_ref, bmb_ref, m_ref, acc_ref, *, lam_init):
    tq = ATTN_TILE
    tk = ATTN_TILE
    n_tiles = qt_ref.shape[0]
    n_streams = 2 * tq // SCORE_COLS
    per_map = n_streams // 2

    m_ref[...] = jnp.full(m_ref.shape, NEG_BIG, F32)
    acc_ref[...] = jnp.zeros_like(acc_ref)
    lam =(jnp.exp(jnp.sum(lq1_ref[...] * lk1_ref[...], keepdims=True))
           - jnp.exp(jnp.sum(lq2_ref[...] * lk2_ref[...], keepdims=True)) + lam_init)

    def scores(qi, kb, s_ref, bm_ref, on_diagonal):
        k_start = _aligned(kb * tk, tk)
        for c in range(n_streams):
            qoff = (c % per_map) * SCORE_COLS
            rows = min(tk, qoff + SCORE_COLS) if on_diagonal else tk
            k_ref = k0_ref if c < per_map else k1_ref
            s = _dot(k_ref[pl.ds(k_start, rows), :], qt_ref[qi, :, qoff:qoff + SCORE_COLS])
            s_ref[c, 0:rows, :] = s
            if not on_diagonal:
                bm_ref[c] = jnp.max(s, axis=0, keepdims=True)

    def softmax_pv(qi, kb, s_ref, bm_ref, on_diagonal):
        for c in range(n_streams):
            qoff = (c % per_map) * SCORE_COLS
            rows = min(tk, qoff + SCORE_COLS) if on_diagonal else tk
            st = s_ref[c, 0:rows, :]
            if on_diagonal:
                krow = lax.broadcasted_iota(jnp.int32, st.shape, 0)
                qcol = lax.broadcasted_iota(jnp.int32, st.shape, 1) + qoff
                st = jnp.where(krow <= qcol, st, NEG_BIG)
                bm = jnp.max(st, axis=0, keepdims=True)
            else:
                bm = bm_ref[c]
            m_old = m_ref[qi, c]
            m_new = jnp.maximum(m_old, bm)
            alpha = jnp.exp2(m_old - m_new)
            pt = jnp.exp2(st - m_new)
            m_ref[qi, c] = m_new
            acc_ref[qi, c] = alpha * acc_ref[qi, c] + _dot(vt_ref[kb, :, 0:rows], pt.astype(BF16))

    def finalize(qi):
        for j in range(per_map):
            a0 = acc_ref[qi, j]
            a1 = acc_ref[qi, per_map + j]
            ot = (a0[0:HEAD_DIM] / a0[HEAD_DIM:HEAD_DIM + 1]
                  - lam * (a1[0:HEAD_DIM] / a1[HEAD_DIM:HEAD_DIM + 1]))
            ms = jnp.mean(ot * ot, axis=0, keepdims=True)
            ot = ot * lax.rsqrt(ms + RMS_EPS) * g_ref[...] * (1.0 - lam_init)
            row0 = _aligned(qi * tq + j * SCORE_COLS, SCORE_COLS)
            o_ref[pl.ds(row0, SCORE_COLS), :] = ot.T.astype(o_ref.dtype)

    bufs = ((sa_ref, bma_ref), (sb_ref, bmb_ref))

    n_full = n_tiles * (n_tiles - 1) // 2
    assert n_full % FULL_UNROLL == 0 and FULL_UNROLL % 2 == 0
    assert n_tiles % DIAG_UNROLL == 0 and DIAG_UNROLL % 2 == 0

    def advance(qi, kb):
        wrap = kb + 1 >= qi
        return jnp.where(wrap, qi + 1, qi), jnp.where(wrap, 0, kb + 1)

    scores(1, 0, sa_ref, bma_ref, False)

    def full_body(_, carry):
        qi, kb = carry
        for step in range(FULL_UNROLL):
            cur, nxt = bufs[step % 2], bufs[(step + 1) % 2]
            nqi, nkb = advance(qi, kb)
            scores(jnp.minimum(nqi, n_tiles - 1), nkb, nxt[0], nxt[1], False)
            softmax_pv(qi, kb, cur[0], cur[1], False)
            qi, kb = nqi, nkb
        return qi, kb

    lax.fori_loop(0, n_full // FULL_UNROLL, full_body, (jnp.int32(1), jnp.int32(0)))

    scores(0, 0, sa_ref, bma_ref, True)

    def diag_body(u, carry):
        for step in range(DIAG_UNROLL):
            cur, nxt = bufs[step % 2], bufs[(step + 1) % 2]
            qi = DIAG_UNROLL * u + step
            nqi = jnp.minimum(qi + 1, n_tiles - 1)
            scores(nqi, nqi, nxt[0], nxt[1], True)
            softmax_pv(qi, qi, cur[0], cur[1], True)
            finalize(qi)
        return carry

    lax.fori_loop(0, n_tiles // DIAG_UNROLL, diag_body, 0)


def _attention(qt, k0, k1, vt, lq1, lk1, lq2, lk2, g_col, lam_init):
    bn, _, s_len, _ = k0.shape
    tq = ATTN_TILE
    n_tiles = s_len // tq
    n_streams = 2 * tq // SCORE_COLS
    vec = pl.BlockSpec((1, QK_DIM), lambda b, h: (0, 0))
    seq = pl.BlockSpec((None, None, s_len, HEAD_DIM), lambda b, h: (b, h, 0, 0))
    return pl.pallas_call(
        functools.partial(_attn_kernel, lam_init=lam_init),
        grid=(bn, HEADS),
        in_specs=[
            vec, vec, vec, vec,
            pl.BlockSpec((HEAD_DIM, 1), lambda b, h: (0, 0)),
            pl.BlockSpec((None, None, n_tiles, HEAD_DIM, tq), lambda b, h: (b, h, 0, 0, 0)),
            seq, seq,
            pl.BlockSpec((None, None, n_tiles, V_ROWS, tq), lambda b, h: (b, h, 0, 0, 0)),
        ],
        out_specs=pl.BlockSpec((None, s_len, HEAD_DIM), lambda b, h: (b, 0, h)),
        out_shape=jax.ShapeDtypeStruct((bn, s_len, HEADS * HEAD_DIM), BF16),
        scratch_shapes=[
            pltpu.VMEM((n_streams, tq, SCORE_COLS), F32),
            pltpu.VMEM((n_streams, tq, SCORE_COLS), F32),
            pltpu.VMEM((n_streams, 1, SCORE_COLS), F32),
            pltpu.VMEM((n_streams, 1, SCORE_COLS), F32),
            pltpu.VMEM((n_tiles, n_streams, 1, SCORE_COLS), F32),
            pltpu.VMEM((n_tiles, n_streams, V_ROWS, SCORE_COLS), F32),
        ],
        compiler_params=pltpu.CompilerParams(
            dimension_semantics=("arbitrary", "arbitrary"),
            vmem_limit_bytes=ATTN_VMEM_LIMIT_BYTES),
        name="diff_attn",
    )(lq1, lk1, lq2, lk2, g_col, qt, k0, k1, vt)


def _final_kernel(x_ref, p_ref, o_ref, szb_ref, sgb_ref, pa_ref, wbb_ref, wout_ref, wpg_ref,
                  wp_ref, lng_ref, lnb_ref, out_ref, *, alpha):
    for r0 in range(0, x_ref.shape[0], SUB_TILE):
        sub = slice(r0, r0 + SUB_TILE)
        x = x_ref[sub, :]
        xb = x.astype(BF16)
        ple = (_sigmoid(_dot(xb, wpg_ref[...]))
               * _dot(p_ref[sub, :].astype(BF16), wp_ref[...]))
        yb = (o_ref[sub, :].astype(F32) * szb_ref[sub, :].astype(F32)).astype(BF16)
        merged = (pa_ref[sub, :].astype(F32)
                  + sgb_ref[sub, :].astype(F32) * _dot(yb, wbb_ref[...]))
        mix = _dot(merged.astype(BF16), wout_ref[...])
        y = alpha * x + mix + ple
        out_ref[sub, :] = _layer_norm(y, lng_ref[...], lnb_ref[...]).astype(out_ref.dtype)


def _final(x2, p2, o2, szb2, sgb2, pa2, wbb, wout, wpg, wp, lng, lnb, alpha):
    n_tok = x2.shape[0]
    tm = TOKEN_TILE
    tile = lambda w: pl.BlockSpec((tm, w), lambda i: (i, 0))
    full = _resident
    return pl.pallas_call(
        functools.partial(_final_kernel, alpha=alpha),
        grid=(n_tok // tm,),
        in_specs=[
            tile(D_MODEL), tile(PLE_DIM), tile(D_MODEL), tile(D_MODEL), tile(D_MODEL), tile(D_MODEL),
            full((D_MODEL, D_MODEL)), full((D_MODEL, D_MODEL)), full((D_MODEL, D_MODEL)),
            full((PLE_DIM, D_MODEL)), full((1, D_MODEL)), full((1, D_MODEL)),
        ],
        out_specs=tile(D_MODEL),
        out_shape=jax.ShapeDtypeStruct((n_tok, D_MODEL), x2.dtype),
        compiler_params=pltpu.CompilerParams(
            dimension_semantics=("arbitrary",), vmem_limit_bytes=VMEM_LIMIT_BYTES),
        name="merge_out",
    )(x2, p2, o2, szb2, sgb2, pa2, wbb, wout, wpg, wp, lng, lnb)


def _rope_rows():
    inv_freq = ROPE_THETA ** (-jnp.arange(0, QK_DIM, 2, dtype=F32) / QK_DIM)
    ones = jnp.ones_like(inv_freq)
    return (jnp.tile(inv_freq, 4).reshape(1, HEAD_DIM),
            jnp.concatenate([-ones, ones, -ones, ones]).reshape(1, HEAD_DIM))


def kernel(x, p, positions, w_in, a_ln_g, a_ln_b, a_w_s, a_b_s, b_lam_q1, b_lam_k1, b_lam_q2,
           b_lam_k2, b_subln_g, w_branch_a, w_branch_b, w_out, w_ple, w_ple_gate, ln_g, ln_b):
    bn, s_len, d = x.shape
    depth = w_in.shape[0]
    n_tok = bn * s_len
    alpha = (2 * depth) ** 0.25
    freq, sign = _rope_rows()
    pos3 = positions.reshape(bn, s_len, 1)
    row = lambda v: v.reshape(1, -1).astype(F32)
    for i in range(depth):
        lam_init = 0.8 - 0.6 * math.exp(-0.3 * i)
        w_all = w_in[i].astype(BF16)
        x2 = x.reshape(n_tok, d)
        pa = _mixer_a(x2, w_all, row(a_ln_g[i]), row(a_ln_b[i]), a_w_s[i], a_b_s[i].T,
                      w_branch_a[i].astype(BF16))
        qt, k0, k1, vt, szb, sgb = _qkv(x, pos3, freq, sign, w_all)
        o = _attention(qt, k0, k1, vt, row(b_lam_q1[i]), row(b_lam_k1[i]), row(b_lam_q2[i]),
                       row(b_lam_k2[i]), b_subln_g[i].reshape(HEAD_DIM, 1).astype(F32), lam_init)
        y = _final(x2, p[i].reshape(n_tok, PLE_DIM), o.reshape(n_tok, d), szb.reshape(n_tok, d),
                   sgb.reshape(n_tok, d), pa, w_branch_b[i].astype(BF16), w_out[i].astype(BF16),
                   w_ple_gate[i].astype(BF16), w_ple[i].astype(BF16), row(ln_g[i]), row(ln_b[i]),
                   alpha)
        x = y.reshape(bn, s_len, d)
    return x
```

```python
import functools
import math

import jax
import jax.numpy as jnp
from jax import lax
from jax.experimental import pallas as pl
from jax.experimental.pallas import tpu as pltpu

F32 = jnp.float32
BF16 = jnp.bfloat16

D_MODEL = 1024
PLE_DIM = 256
CHUNK = 128
A_GROUPS = 8
HEADS = 8
QK_DIM = 64
HEAD_DIM = 2 * QK_DIM
ROPE_THETA = 10000.0
LN_EPS = 1e-5
RMS_EPS = 1e-5
NEG_BIG = -1e30
(IN_COL_UA, IN_COL_VA, IN_COL_ZA, IN_COL_Q, IN_COL_K, IN_COL_V, IN_COL_ZB, IN_COL_GA,
 IN_COL_GB) = range(9)

TOKEN_TILE = 1024
SUB_TILE = 256
COL_TILE = 256
ATTN_TILE = 512
SCORE_COLS = 256
V_ROWS = HEAD_DIM + 16
FULL_UNROLL = 24
DIAG_UNROLL = 4
VMEM_LIMIT_BYTES = 48 * 1024 * 1024
ATTN_VMEM_LIMIT_BYTES = 56 * 1024 * 1024


def _dot(a, b):
    return jnp.dot(a, b, preferred_element_type=F32)


def _resident(shape):
    return pl.BlockSpec(shape, lambda *_: (0,) * len(shape), pipeline_mode=pl.Buffered(1))


def _resident_cols(j):
    return pl.BlockSpec((D_MODEL, D_MODEL), lambda *_: (0, j), pipeline_mode=pl.Buffered(1))


def _aligned(index, multiple):
    return index if isinstance(index, int) else pl.multiple_of(index, multiple)


def _gelu(x):
    return 0.5 * x * (1.0 + lax.erf(x * math.sqrt(0.5)))


def _sigmoid(x):
    return jax.nn.sigmoid(x)


def _layer_norm(x, g, b):
    mu = jnp.mean(x, axis=-1, keepdims=True)
    xc = x - mu
    var = jnp.mean(xc * xc, axis=-1, keepdims=True)
    return xc * lax.rsqrt(var + LN_EPS) * g + b


def _mixer_a_kernel(x_ref, wu_ref, wv_ref, wz_ref, wg_ref, lng_ref, lnb_ref, ws_ref, bst_ref,
                    wba_ref, out_ref, vb_ref, ya_ref):
    tm = x_ref.shape[0]
    row = lax.broadcasted_iota(jnp.int32, (CHUNK, CHUNK), 0)
    col = lax.broadcasted_iota(jnp.int32, (CHUNK, CHUNK), 1)
    causal = col <= row
    groups_per_block = COL_TILE // CHUNK
    for r0 in range(0, tm, SUB_TILE):
        sub = slice(r0, r0 + SUB_TILE)
        xb = x_ref[sub, :].astype(BF16)
        va = _gelu(_dot(xb, wv_ref[...]))
        ga = _dot(xb, wg_ref[...])
        vb_ref[sub, :] = _layer_norm(va, lng_ref[...], lnb_ref[...]).astype(BF16)
        for cb in range(D_MODEL // COL_TILE):
            cols = slice(cb * COL_TILE, (cb + 1) * COL_TILE)
            u = _gelu(_dot(xb, wu_ref[:, cols]))
            z = _dot(xb, wz_ref[:, cols])
            sz = z * _sigmoid(z)
            for gi in range(groups_per_block):
                g = cb * groups_per_block + gi
                gcols = slice(g * CHUNK, (g + 1) * CHUNK)
                lcols = slice(gi * CHUNK, (gi + 1) * CHUNK)
                ws_g = jnp.where(causal, ws_ref[g], 0.0).astype(BF16)
                bias = bst_ref[:, g:g + 1]
                n_chunks = SUB_TILE // CHUNK
                v_wide = jnp.concatenate(
                    [vb_ref[r0 + n * CHUNK:r0 + (n + 1) * CHUNK, gcols] for n in range(n_chunks)],
                    axis=1)
                sa_wide = _dot(ws_g, v_wide) + bias
                for n in range(n_chunks):
                    lrows = slice(n * CHUNK, (n + 1) * CHUNK)
                    sa = sa_wide[:, lrows]
                    ya_ref[r0 + n * CHUNK:r0 + (n + 1) * CHUNK, gcols] = (
                        u[lrows, lcols] * sa * sz[lrows, lcols]).astype(BF16)
        pa = _dot(ya_ref[sub, :], wba_ref[...])
        out_ref[sub, :] = (_sigmoid(ga) * pa).astype(out_ref.dtype)


def _mixer_a(x2, w_all, lng, lnb, ws, bst, wba):
    n_tok = x2.shape[0]
    tm = TOKEN_TILE
    full = _resident
    return pl.pallas_call(
        _mixer_a_kernel,
        grid=(n_tok // tm,),
        in_specs=[
            pl.BlockSpec((tm, D_MODEL), lambda i: (i, 0)),
            _resident_cols(IN_COL_UA), _resident_cols(IN_COL_VA), _resident_cols(IN_COL_ZA),
            _resident_cols(IN_COL_GA),
            full((1, D_MODEL)), full((1, D_MODEL)),
            full((A_GROUPS, CHUNK, CHUNK)), full((CHUNK, A_GROUPS)),
            full((D_MODEL, D_MODEL)),
        ],
        out_specs=pl.BlockSpec((tm, D_MODEL), lambda i: (i, 0)),
        out_shape=jax.ShapeDtypeStruct((n_tok, D_MODEL), BF16),
        scratch_shapes=[pltpu.VMEM((tm, D_MODEL), BF16), pltpu.VMEM((tm, D_MODEL), BF16)],
        compiler_params=pltpu.CompilerParams(
            dimension_semantics=("arbitrary",), vmem_limit_bytes=VMEM_LIMIT_BYTES),
        name="mixer_a",
    )(x2, w_all, w_all, w_all, w_all, lng, lnb, ws, bst, wba)


def _qkv_kernel(x_ref, pos_ref, freq_ref, sign_ref, wq_ref, wk_ref, wv_ref, wzb_ref, wgb_ref,
                qt_ref, k0_ref, k1_ref, vt_ref, szb_ref, sgb_ref, *, q_scale):
    tm = x_ref.shape[0]
    lane = lax.broadcasted_iota(jnp.int32, (SUB_TILE, HEAD_DIM), 1)
    in_map0 = (lane & QK_DIM) == 0
    in_lo = (lane & (QK_DIM // 2)) == 0
    ta = vt_ref.shape[-1]
    for h in range(HEADS):
        for tile in range(tm // ta):
            vt_ref[h, tile, HEAD_DIM:V_ROWS, :] = jnp.ones((V_ROWS - HEAD_DIM, ta), BF16)

    for r0 in range(0, tm, SUB_TILE):
        sub = slice(r0, r0 + SUB_TILE)
        tile, tcols = r0 // ta, slice(r0 % ta, r0 % ta + SUB_TILE)
        xb = x_ref[sub, :].astype(BF16)

        v = _dot(xb, wv_ref[...])
        for h in range(HEADS):
            hs = slice(h * HEAD_DIM, (h + 1) * HEAD_DIM)
            vt_ref[h, tile, 0:HEAD_DIM, tcols] = v[:, hs].T.astype(BF16)

        ang = pos_ref[sub, :].astype(F32) * freq_ref[...]
        cos = jnp.cos(ang)
        sin = jnp.sin(ang) * sign_ref[...]

        def rope(t):
            up = pltpu.roll(t, HEAD_DIM - QK_DIM // 2, 1)
            down = pltpu.roll(t, QK_DIM // 2, 1)
            return t * cos + jnp.where(in_lo, up, down) * sin

        q = _dot(xb, wq_ref[...])
        for h in range(HEADS):
            hs = slice(h * HEAD_DIM, (h + 1) * HEAD_DIM)
            qt_ref[h, tile, :, tcols] = (rope(q[:, hs]) * q_scale).T.astype(BF16)
        k = _dot(xb, wk_ref[...])
        for h in range(HEADS):
            hs = slice(h * HEAD_DIM, (h + 1) * HEAD_DIM)
            kr = rope(k[:, hs])
            k0_ref[h, sub, :] = jnp.where(in_map0, kr, 0.0).astype(BF16)
            k1_ref[h, sub, :] = jnp.where(in_map0, 0.0, kr).astype(BF16)

        zb = _dot(xb, wzb_ref[...])
        szb_ref[sub, :] = (zb * _sigmoid(zb)).astype(BF16)
        gb = _dot(xb, wgb_ref[...])
        sgb_ref[sub, :] = _sigmoid(gb).astype(BF16)


def _qkv(x3, pos3, freq, sign, w_all):
    bn, s_len, _ = x3.shape
    tm = TOKEN_TILE
    ta = ATTN_TILE
    per_step = tm // ta
    lane_row = _resident((1, HEAD_DIM))
    return pl.pallas_call(
        functools.partial(_qkv_kernel, q_scale=QK_DIM ** -0.5 * math.log2(math.e)),
        grid=(bn, s_len // tm),
        in_specs=[
            pl.BlockSpec((None, tm, D_MODEL), lambda b, i: (b, i, 0)),
            pl.BlockSpec((None, tm, 1), lambda b, i: (b, i, 0)),
            lane_row, lane_row,
            _resident_cols(IN_COL_Q), _resident_cols(IN_COL_K), _resident_cols(IN_COL_V),
            _resident_cols(IN_COL_ZB), _resident_cols(IN_COL_GB),
        ],
        out_specs=[
            pl.BlockSpec((None, HEADS, per_step, HEAD_DIM, ta), lambda b, i: (b, 0, i, 0, 0)),
            pl.BlockSpec((None, HEADS, tm, HEAD_DIM), lambda b, i: (b, 0, i, 0)),
            pl.BlockSpec((None, HEADS, tm, HEAD_DIM), lambda b, i: (b, 0, i, 0)),
            pl.BlockSpec((None, HEADS, per_step, V_ROWS, ta), lambda b, i: (b, 0, i, 0, 0)),
            pl.BlockSpec((None, tm, D_MODEL), lambda b, i: (b, i, 0)),
            pl.BlockSpec((None, tm, D_MODEL), lambda b, i: (b, i, 0)),
        ],
        out_shape=[
            jax.ShapeDtypeStruct((bn, HEADS, s_len // ta, HEAD_DIM, ta), BF16),
            jax.ShapeDtypeStruct((bn, HEADS, s_len, HEAD_DIM), BF16),
            jax.ShapeDtypeStruct((bn, HEADS, s_len, HEAD_DIM), BF16),
            jax.ShapeDtypeStruct((bn, HEADS, s_len // ta, V_ROWS, ta), BF16),
            jax.ShapeDtypeStruct((bn, s_len, D_MODEL), BF16),
            jax.ShapeDtypeStruct((bn, s_len, D_MODEL), BF16),
        ],
        compiler_params=pltpu.CompilerParams(
            dimension_semantics=("arbitrary", "arbitrary"), vmem_limit_bytes=VMEM_LIMIT_BYTES),
        name="qkv_proj",
    )(x3, pos3, freq, sign, w_all, w_all, w_all, w_all, w_all)


def _attn_kernel(lq1_ref, lk1_ref, lq2_ref, lk2_ref, g_ref, qt_ref, k0_ref, k1_ref, vt_ref, o_ref,
                 sa_ref, sb_ref, bma_ref, bmb_ref, m_ref, acc_ref, *, lam_init):
    tq = ATTN_TILE
    tk = ATTN_TILE
    n_tiles = qt_ref.shape[0]
    n_streams = 2 * tq // SCORE_COLS
    per_map = n_streams // 2

    m_ref[...] = jnp.full(m_ref.shape, NEG_BIG, F32)
    acc_ref[...] = jnp.zeros_like(acc_ref)
    lam = (jnp.exp(jnp.sum(lq1_ref[...] * lk1_ref[...], keepdims=True))
           - jnp.exp(jnp.sum(lq2_ref[...] * lk2_ref[...], keepdims=True)) + lam_init)

    def scores(qi, kb, s_ref, bm_ref, on_diagonal):
        k_start = _aligned(kb * tk, tk)
        for c in range(n_streams):
            qoff = (c % per_map) * SCORE_COLS
            rows = min(tk, qoff + SCORE_COLS) if on_diagonal else tk
            k_ref = k0_ref if c < per_map else k1_ref
            s = _dot(k_ref[pl.ds(k_start, rows), :], qt_ref[qi, :, qoff:qoff + SCORE_COLS])
            s_ref[c, 0:rows, :] = s
            if not on_diagonal:
                bm_ref[c] = jnp.max(s, axis=0, keepdims=True)

    def softmax_pv(qi, kb, s_ref, bm_ref, on_diagonal):
        for c in range(n_streams):
            qoff = (c % per_map) * SCORE_COLS
            rows = min(tk, qoff + SCORE_COLS) if on_diagonal else tk
            st = s_ref[c, 0:rows, :]
            if on_diagonal:
                krow = lax.broadcasted_iota(jnp.int32, st.shape, 0)
                qcol = lax.broadcasted_iota(jnp.int32, st.shape, 1) + qoff
                st = jnp.where(krow <= qcol, st, NEG_BIG)
                bm = jnp.max(st, axis=0, keepdims=True)
            else:
                bm = bm_ref[c]
            m_old = m_ref[qi, c]
            m_new = jnp.maximum(m_old, bm)
            alpha = jnp.exp2(m_old - m_new)
            pt = jnp.exp2(st - m_new)
            m_ref[qi, c] = m_new
            acc_ref[qi, c] = alpha * acc_ref[qi, c] + _dot(vt_ref[kb, :, 0:rows], pt.astype(BF16))

    def finalize(qi):
        for j in range(per_map):
            a0 = acc_ref[qi, j]
            a1 = acc_ref[qi, per_map + j]
            ot = (a0[0:HEAD_DIM] / a0[HEAD_DIM:HEAD_DIM + 1]
                  - lam * (a1[0:HEAD_DIM] / a1[HEAD_DIM:HEAD_DIM + 1]))
            ms = jnp.mean(ot * ot, axis=0, keepdims=True)
            ot = ot * lax.rsqrt(ms + RMS_EPS) * g_ref[...] * (1.0 - lam_init)
            row0 = _aligned(qi * tq + j * SCORE_COLS, SCORE_COLS)
            o_ref[pl.ds(row0, SCORE_COLS), :] = ot.T.astype(o_ref.dtype)

    bufs = ((sa_ref, bma_ref), (sb_ref, bmb_ref))

    n_full = n_tiles * (n_tiles - 1) // 2
    assert n_full % FULL_UNROLL == 0 and FULL_UNROLL % 2 == 0
    assert n_tiles % DIAG_UNROLL == 0 and DIAG_UNROLL % 2 == 0

    def advance(qi, kb):
        wrap = kb + 1 >= qi
        return jnp.where(wrap, qi + 1, qi), jnp.where(wrap, 0, kb + 1)

    scores(1, 0, sa_ref, bma_ref, False)

    def full_body(_, carry):
        qi, kb = carry
        for step in range(FULL_UNROLL):
            cur, nxt = bufs[step % 2], bufs[(step + 1) % 2]
            nqi, nkb = advance(qi, kb)
            scores(jnp.minimum(nqi, n_tiles - 1), nkb, nxt[0], nxt[1], False)
            softmax_pv(qi, kb, cur[0], cur[1], False)
            qi, kb = nqi, nkb
        return qi, kb

    lax.fori_loop(0, n_full // FULL_UNROLL, full_body, (jnp.int32(1), jnp.int32(0)))

    scores(0, 0, sa_ref, bma_ref, True)

    def diag_body(u, carry):
        for step in range(DIAG_UNROLL):
            cur, nxt = bufs[step % 2], bufs[(step + 1) % 2]
            qi = DIAG_UNROLL * u + step
            nqi = jnp.minimum(qi + 1, n_tiles - 1)
            scores(nqi, nqi, nxt[0], nxt[1], True)
            softmax_pv(qi, qi, cur[0], cur[1], True)
            finalize(qi)
        return carry

    lax.fori_loop(0, n_tiles // DIAG_UNROLL, diag_body, 0)


def _attention(qt, k0, k1, vt, lq1, lk1, lq2, lk2, g_col, lam_init):
    bn, _, s_len, _ = k0.shape
    tq = ATTN_TILE
    n_tiles = s_len // tq
    n_streams = 2 * tq // SCORE_COLS
    vec = pl.BlockSpec((1, QK_DIM), lambda b, h: (0, 0))
    seq = pl.BlockSpec((None, None, s_len, HEAD_DIM), lambda b, h: (b, h, 0, 0))
    return pl.pallas_call(
        functools.partial(_attn_kernel, lam_init=lam_init),
        grid=(bn, HEADS),
        in_specs=[
            vec, vec, vec, vec,
            pl.BlockSpec((HEAD_DIM, 1), lambda b, h: (0, 0)),
            pl.BlockSpec((None, None, n_tiles, HEAD_DIM, tq), lambda b, h: (b, h, 0, 0, 0)),
            seq, seq,
            pl.BlockSpec((None, None, n_tiles, V_ROWS, tq), lambda b, h: (b, h, 0, 0, 0)),
        ],
        out_specs=pl.BlockSpec((None, s_len, HEAD_DIM), lambda b, h: (b, 0, h)),
        out_shape=jax.ShapeDtypeStruct((bn, s_len, HEADS * HEAD_DIM), BF16),
        scratch_shapes=[
            pltpu.VMEM((n_streams, tq, SCORE_COLS), F32),
            pltpu.VMEM((n_streams, tq, SCORE_COLS), F32),
            pltpu.VMEM((n_streams, 1, SCORE_COLS), F32),
            pltpu.VMEM((n_streams, 1, SCORE_COLS), F32),
            pltpu.VMEM((n_tiles, n_streams, 1, SCORE_COLS), F32),
            pltpu.VMEM((n_tiles, n_streams, V_ROWS, SCORE_COLS), F32),
        ],
        compiler_params=pltpu.CompilerParams(
            dimension_semantics=("arbitrary", "arbitrary"),
            vmem_limit_bytes=ATTN_VMEM_LIMIT_BYTES),
        name="diff_attn",
    )(lq1, lk1, lq2, lk2, g_col, qt, k0, k1, vt)


def _final_kernel(x_ref, p_ref, o_ref, szb_ref, sgb_ref, pa_ref, wbb_ref, wout_ref, wpg_ref,
                  wp_ref, lng_ref, lnb_ref, out_ref, *, alpha):
    for r0 in range(0, x_ref.shape[0], SUB_TILE):
        sub = slice(r0, r0 + SUB_TILE)
        x = x_ref[sub, :]
        xb = x.astype(BF16)
        ple = (_sigmoid(_dot(xb, wpg_ref[...]))
               * _dot(p_ref[sub, :].astype(BF16), wp_ref[...]))
        yb = (o_ref[sub, :].astype(F32) * szb_ref[sub, :].astype(F32)).astype(BF16)
        merged = (pa_ref[sub, :].astype(F32)
                  + sgb_ref[sub, :].astype(F32) * _dot(yb, wbb_ref[...]))
        mix = _dot(merged.astype(BF16), wout_ref[...])
        y = alpha * x + mix + ple
        out_ref[sub, :] = _layer_norm(y, lng_ref[...], lnb_ref[...]).astype(out_ref.dtype)


def _final(x2, p2, o2, szb2, sgb2, pa2, wbb, wout, wpg, wp, lng, lnb, alpha):
    n_tok = x2.shape[0]
    tm = TOKEN_TILE
    tile = lambda w: pl.BlockSpec((tm, w), lambda i: (i, 0))
    full = _resident
    return pl.pallas_call(
        functools.partial(_final_kernel, alpha=alpha),
        grid=(n_tok // tm,),
        in_specs=[
            tile(D_MODEL), tile(PLE_DIM), tile(D_MODEL), tile(D_MODEL), tile(D_MODEL), tile(D_MODEL),
            full((D_MODEL, D_MODEL)), full((D_MODEL, D_MODEL)), full((D_MODEL, D_MODEL)),
            full((PLE_DIM, D_MODEL)), full((1, D_MODEL)), full((1, D_MODEL)),
        ],
        out_specs=tile(D_MODEL),
        out_shape=jax.ShapeDtypeStruct((n_tok, D_MODEL), x2.dtype),
        compiler_params=pltpu.CompilerParams(
            dimension_semantics=("arbitrary",), vmem_limit_bytes=VMEM_LIMIT_BYTES),
        name="merge_out",
    )(x2, p2, o2, szb2, sgb2, pa2, wbb, wout, wpg, wp, lng, lnb)


def _rope_rows():
    inv_freq = ROPE_THETA ** (-jnp.arange(0, QK_DIM, 2, dtype=F32) / QK_DIM)
    ones = jnp.ones_like(inv_freq)
    return (jnp.tile(inv_freq, 4).reshape(1, HEAD_DIM),
            jnp.concatenate([-ones, ones, -ones, ones]).reshape(1, HEAD_DIM))


def kernel(x, p, positions, w_in, a_ln_g, a_ln_b, a_w_s, a_b_s, b_lam_q1, b_lam_k1, b_lam_q2,
           b_lam_k2, b_subln_g, w_branch_a, w_branch_b, w_out, w_ple, w_ple_gate, ln_g, ln_b):
    bn, s_len, d = x.shape
    depth = w_in.shape[0]
    n_tok = bn * s_len
    alpha = (2 * depth) ** 0.25
    freq, sign = _rope_rows()
    pos3 = positions.reshape(bn, s_len, 1)
    row = lambda v: v.reshape(1, -1).astype(F32)
    for i in range(depth):
        lam_init = 0.8 - 0.6 * math.exp(-0.3 * i)
        w_all = w_in[i].astype(BF16)
        x2 = x.reshape(n_tok, d)
        pa = _mixer_a(x2, w_all, row(a_ln_g[i]), row(a_ln_b[i]), a_w_s[i], a_b_s[i].T,
                      w_branch_a[i].astype(BF16))
        qt, k0, k1, vt, szb, sgb = _qkv(x, pos3, freq, sign, w_all)
        o = _attention(qt, k0, k1, vt, row(b_lam_q1[i]), row(b_lam_k1[i]), row(b_lam_q2[i]),
                       row(b_lam_k2[i]), b_subln_g[i].reshape(HEAD_DIM, 1).astype(F32), lam_init)
        y = _final(x2, p[i].reshape(n_tok, PLE_DIM), o.reshape(n_tok, d), szb.reshape(n_tok, d),
                   sgb.reshape(n_tok, d), pa, w_branch_b[i].astype(BF16), w_out[i].astype(BF16),
                   w_ple_gate[i].astype(BF16), w_ple[i].astype(BF16), row(ln_g[i]), row(ln_b[i]),
                   alpha)
        x = y.reshape(bn, s_len, d)
    return x
```

```python
import functools
import math

import jax
import jax.numpy as jnp
from jax import lax
from jax.experimental import pallas as pl
from jax.experimental.pallas import tpu as pltpu

F32 = jnp.float32
BF16 = jnp.bfloat16

D_MODEL = 1024
PLE_DIM = 256
CHUNK = 128
A_GROUPS = 8
HEADS = 8
QK_DIM = 64
HEAD_DIM = 2 * QK_DIM
ROPE_THETA = 10000.0
LN_EPS = 1e-5
RMS_EPS = 1e-5
NEG_BIG = -1e30
(IN_COL_UA, IN_COL_VA, IN_COL_ZA, IN_COL_Q, IN_COL_K, IN_COL_V, IN_COL_ZB, IN_COL_GA,
 IN_COL_GB) = range(9)

TOKEN_TILE = 1024
SUB_TILE = 256
COL_TILE = 256
ATTN_TILE = 512
SCORE_COLS = 256
V_ROWS = HEAD_DIM + 16
FULL_UNROLL = 24
DIAG_UNROLL = 8
VMEM_LIMIT_BYTES = 48 * 1024 * 1024
ATTN_VMEM_LIMIT_BYTES = 56 * 1024 * 1024


def _dot(a, b):
    return jnp.dot(a, b, preferred_element_type=F32)


def _resident(shape):
    return pl.BlockSpec(shape, lambda *_: (0,) * len(shape), pipeline_mode=pl.Buffered(1))


def _resident_cols(j):
    return pl.BlockSpec((D_MODEL, D_MODEL), lambda *_: (0, j), pipeline_mode=pl.Buffered(1))


def _aligned(index, multiple):
    return index if isinstance(index, int) else pl.multiple_of(index, multiple)


def _gelu(x):
    return 0.5 * x * (1.0 + lax.erf(x * math.sqrt(0.5)))


def _sigmoid(x):
    return jax.nn.sigmoid(x)


def _layer_norm(x, g, b):
    mu = jnp.mean(x, axis=-1, keepdims=True)
    xc = x - mu
    var = jnp.mean(xc * xc, axis=-1, keepdims=True)
    return xc * lax.rsqrt(var + LN_EPS) * g + b


def _mixer_a_kernel(x_ref, wu_ref, wv_ref, wz_ref, wg_ref, lng_ref, lnb_ref, ws_ref, bst_ref,
                    wba_ref, out_ref, vb_ref, ya_ref):
    tm = x_ref.shape[0]
    row = lax.broadcasted_iota(jnp.int32, (CHUNK, CHUNK), 0)
    col = lax.broadcasted_iota(jnp.int32, (CHUNK, CHUNK), 1)
    causal = col <= row
    groups_per_block = COL_TILE // CHUNK
    for r0 in range(0, tm, SUB_TILE):
        sub = slice(r0, r0 + SUB_TILE)
        xb = x_ref[sub, :].astype(BF16)
        va = _gelu(_dot(xb, wv_ref[...]))
        ga = _dot(xb, wg_ref[...])
        vb_ref[sub, :] = _layer_norm(va, lng_ref[...], lnb_ref[...]).astype(BF16)
        for cb in range(D_MODEL // COL_TILE):
            cols = slice(cb * COL_TILE, (cb + 1) * COL_TILE)
            u = _gelu(_dot(xb, wu_ref[:, cols]))
            z = _dot(xb, wz_ref[:, cols])
            sz = z * _sigmoid(z)
            for gi in range(groups_per_block):
                g = cb * groups_per_block + gi
                gcols = slice(g * CHUNK, (g + 1) * CHUNK)
                lcols = slice(gi * CHUNK, (gi + 1) * CHUNK)
                ws_g = jnp.where(causal, ws_ref[g], 0.0).astype(BF16)
                bias = bst_ref[:, g:g + 1]
                n_chunks = SUB_TILE // CHUNK
                v_wide = jnp.concatenate(
                    [vb_ref[r0 + n * CHUNK:r0 + (n + 1) * CHUNK, gcols] for n in range(n_chunks)],
                    axis=1)
                sa_wide = _dot(ws_g, v_wide) + bias
                for n in range(n_chunks):
                    lrows = slice(n * CHUNK, (n + 1) * CHUNK)
                    sa = sa_wide[:, lrows]
                    ya_ref[r0 + n * CHUNK:r0 + (n + 1) * CHUNK, gcols] = (
                        u[lrows, lcols] * sa * sz[lrows, lcols]).astype(BF16)
        pa = _dot(ya_ref[sub, :], wba_ref[...])
        out_ref[sub, :] = (_sigmoid(ga) * pa).astype(out_ref.dtype)


def _mixer_a(x2, w_all, lng, lnb, ws, bst, wba):
    n_tok = x2.shape[0]
    tm = TOKEN_TILE
    full = _resident
    return pl.pallas_call(
        _mixer_a_kernel,
        grid=(n_tok // tm,),
        in_specs=[
            pl.BlockSpec((tm, D_MODEL), lambda i: (i, 0)),
            _resident_cols(IN_COL_UA), _resident_cols(IN_COL_VA), _resident_cols(IN_COL_ZA),
            _resident_cols(IN_COL_GA),
            full((1, D_MODEL)), full((1, D_MODEL)),
            full((A_GROUPS, CHUNK, CHUNK)), full((CHUNK, A_GROUPS)),
            full((D_MODEL, D_MODEL)),
        ],
        out_specs=pl.BlockSpec((tm, D_MODEL), lambda i: (i, 0)),
        out_shape=jax.ShapeDtypeStruct((n_tok, D_MODEL), BF16),
        scratch_shapes=[pltpu.VMEM((tm, D_MODEL), BF16), pltpu.VMEM((tm, D_MODEL), BF16)],
        compiler_params=pltpu.CompilerParams(
            dimension_semantics=("arbitrary",), vmem_limit_bytes=VMEM_LIMIT_BYTES),
        name="mixer_a",
    )(x2, w_all, w_all, w_all, w_all, lng, lnb, ws, bst, wba)


def _qkv_kernel(x_ref, pos_ref, freq_ref, sign_ref, wq_ref, wk_ref, wv_ref, wzb_ref, wgb_ref,
                qt_ref, k0_ref, k1_ref, vt_ref, szb_ref, sgb_ref, *, q_scale):
    tm = x_ref.shape[0]
    lane = lax.broadcasted_iota(jnp.int32, (SUB_TILE, HEAD_DIM), 1)
    in_map0 = (lane & QK_DIM) == 0
    in_lo = (lane & (QK_DIM // 2)) == 0
    ta = vt_ref.shape[-1]
    for h in range(HEADS):
        for tile in range(tm // ta):
            vt_ref[h, tile, HEAD_DIM:V_ROWS, :] = jnp.ones((V_ROWS - HEAD_DIM, ta), BF16)

    for r0 in range(0, tm, SUB_TILE):
        sub = slice(r0, r0 + SUB_TILE)
        tile, tcols = r0 // ta, slice(r0 % ta, r0 % ta + SUB_TILE)
        xb = x_ref[sub, :].astype(BF16)

        def gates():
            zb = _dot(xb, wzb_ref[...])
            szb_ref[sub, :] = (zb * _sigmoid(zb)).astype(BF16)
            gb = _dot(xb, wgb_ref[...])
            sgb_ref[sub, :] = _sigmoid(gb).astype(BF16)

        v = _dot(xb, wv_ref[...])
        for h in range(HEADS):
            hs = slice(h * HEAD_DIM, (h + 1) * HEAD_DIM)
            vt_ref[h, tile, 0:HEAD_DIM, tcols] = v[:, hs].astype(BF16).T
        if r0 == 0:
            gates()

        ang = pos_ref[sub, :].astype(F32) * freq_ref[...]
        cos = jnp.cos(ang)
        sin = jnp.sin(ang) * sign_ref[...]

        def rope(t):
            up = pltpu.roll(t, HEAD_DIM - QK_DIM // 2, 1)
            down = pltpu.roll(t, QK_DIM // 2, 1)
            return t * cos + jnp.where(in_lo, up, down) * sin

        q = _dot(xb, wq_ref[...])
        for h in range(HEADS):
            hs = slice(h * HEAD_DIM, (h + 1) * HEAD_DIM)
            qt_ref[h, tile, :, tcols] = (rope(q[:, hs]) * q_scale).astype(BF16).T
        k = _dot(xb, wk_ref[...])
        for h in range(HEADS):
            hs = slice(h * HEAD_DIM, (h + 1) * HEAD_DIM)
            kr = rope(k[:, hs])
            k0_ref[h, sub, :] = jnp.where(in_map0, kr, 0.0).astype(BF16)
            k1_ref[h, sub, :] = jnp.where(in_map0, 0.0, kr).astype(BF16)

        if r0 != 0:
            gates()


def _qkv(x3, pos3, freq, sign, w_all):
    bn, s_len, _ = x3.shape
    tm = TOKEN_TILE
    ta = ATTN_TILE
    per_step = tm // ta
    lane_row = _resident((1, HEAD_DIM))
    return pl.pallas_call(
        functools.partial(_qkv_kernel, q_scale=QK_DIM ** -0.5 * math.log2(math.e)),
        grid=(bn, s_len // tm),
        in_specs=[
            pl.BlockSpec((None, tm, D_MODEL), lambda b, i: (b, i, 0)),
            pl.BlockSpec((None, tm, 1), lambda b, i: (b, i, 0)),
            lane_row, lane_row,
            _resident_cols(IN_COL_Q), _resident_cols(IN_COL_K), _resident_cols(IN_COL_V),
            _resident_cols(IN_COL_ZB), _resident_cols(IN_COL_GB),
        ],
        out_specs=[
            pl.BlockSpec((None, HEADS, per_step, HEAD_DIM, ta), lambda b, i: (b, 0, i, 0, 0)),
            pl.BlockSpec((None, HEADS, tm, HEAD_DIM), lambda b, i: (b, 0, i, 0)),
            pl.BlockSpec((None, HEADS, tm, HEAD_DIM), lambda b, i: (b, 0, i, 0)),
            pl.BlockSpec((None, HEADS, per_step, V_ROWS, ta), lambda b, i: (b, 0, i, 0, 0)),
            pl.BlockSpec((None, tm, D_MODEL), lambda b, i: (b, i, 0)),
            pl.BlockSpec((None, tm, D_MODEL), lambda b, i: (b, i, 0)),
        ],
        out_shape=[
            jax.ShapeDtypeStruct((bn, HEADS, s_len // ta, HEAD_DIM, ta), BF16),
            jax.ShapeDtypeStruct((bn, HEADS, s_len, HEAD_DIM), BF16),
            jax.ShapeDtypeStruct((bn, HEADS, s_len, HEAD_DIM), BF16),
            jax.ShapeDtypeStruct((bn, HEADS, s_len // ta, V_ROWS, ta), BF16),
            jax.ShapeDtypeStruct((bn, s_len, D_MODEL), BF16),
            jax.ShapeDtypeStruct((bn, s_len, D_MODEL), BF16),
        ],
        compiler_params=pltpu.CompilerParams(
            dimension_semantics=("arbitrary", "arbitrary"), vmem_limit_bytes=VMEM_LIMIT_BYTES),
        name="qkv_proj",
    )(x3, pos3, freq, sign, w_all, w_all, w_all, w_all, w_all)


def _attn_kernel(lq1_ref, lk1_ref, lq2_ref, lk2_ref, g_ref, qt_ref, k0_ref, k1_ref, vt_ref, o_ref,
                 sa_ref, sb_ref, bma_ref, bmb_ref, m_ref, acc_ref, *, lam_init):
    tq = ATTN_TILE
    tk = ATTN_TILE
    n_tiles = qt_ref.shape[0]
    n_streams = 2 * tq // SCORE_COLS
    per_map = n_streams // 2

    m_ref[...] = jnp.full(m_ref.shape, NEG_BIG, F32)
    acc_ref[...] = jnp.zeros_like(acc_ref)
    lam = (jnp.exp(jnp.sum(lq1_ref[...] * lk1_ref[...], keepdims=True))
           - jnp.exp(jnp.sum(lq2_ref[...] * lk2_ref[...], keepdims=True)) + lam_init)

    def scores(qi, kb, s_ref, bm_ref, on_diagonal, streams=range(n_streams)):
        k_start = _aligned(kb * tk, tk)
        for c in streams:
            qoff = (c % per_map) * SCORE_COLS
            rows = min(tk, qoff + SCORE_COLS) if on_diagonal else tk
            k_ref = k0_ref if c < per_map else k1_ref
            s = _dot(k_ref[pl.ds(k_start, rows), :], qt_ref[qi, :, qoff:qoff + SCORE_COLS])
            s_ref[c, 0:rows, :] = s
            if not on_diagonal:
                bm_ref[c] = jnp.max(s, axis=0, keepdims=True)

    def softmax_pv(qi, kb, s_ref, bm_ref, on_diagonal, streams=range(n_streams)):
        for c in streams:
            qoff = (c % per_map) * SCORE_COLS
            rows = min(tk, qoff + SCORE_COLS) if on_diagonal else tk
            st = s_ref[c, 0:rows, :]
            if on_diagonal:
                krow = lax.broadcasted_iota(jnp.int32, st.shape, 0)
                qcol = lax.broadcasted_iota(jnp.int32, st.shape, 1) + qoff
                st = jnp.where(krow <= qcol, st, NEG_BIG)
                bm = jnp.max(st, axis=0, keepdims=True)
            else:
                bm = bm_ref[c]
            m_old = m_ref[qi, c]
            m_new = jnp.maximum(m_old, bm)
            alpha = jnp.exp2(m_old - m_new)
            pt = jnp.exp2(st - m_new)
            m_ref[qi, c] = m_new
            acc_ref[qi, c] = alpha * acc_ref[qi, c] + _dot(vt_ref[kb, :, 0:rows], pt.astype(BF16))

    def finalize(qi):
        for j in range(per_map):
            a0 = acc_ref[qi, j]
            a1 = acc_ref[qi, per_map + j]
            ot = (a0[0:HEAD_DIM] / a0[HEAD_DIM:HEAD_DIM + 1]
                  - lam * (a1[0:HEAD_DIM] / a1[HEAD_DIM:HEAD_DIM + 1]))
            ms = jnp.mean(ot * ot, axis=0, keepdims=True)
            ot = ot * lax.rsqrt(ms + RMS_EPS) * g_ref[...] * (1.0 - lam_init)
            row0 = _aligned(qi * tq + j * SCORE_COLS, SCORE_COLS)
            o_ref[pl.ds(row0, SCORE_COLS), :] = ot.T.astype(o_ref.dtype)

    bufs = ((sa_ref, bma_ref), (sb_ref, bmb_ref))

    n_full = n_tiles * (n_tiles - 1) // 2
    assert n_full % FULL_UNROLL == 0 and FULL_UNROLL % 2 == 0
    assert n_tiles % DIAG_UNROLL == 0 and DIAG_UNROLL % 2 == 0

    def advance(qi, kb):
        wrap = kb + 1 >= qi
        return jnp.where(wrap, qi + 1, qi), jnp.where(wrap, 0, kb + 1)

    scores(1, 0, sa_ref, bma_ref, False)

    def full_body(_, carry):
        qi, kb = carry
        for step in range(FULL_UNROLL):
            cur, nxt = bufs[step % 2], bufs[(step + 1) % 2]
            nqi, nkb = advance(qi, kb)
            for c in range(n_streams):
                scores(jnp.minimum(nqi, n_tiles - 1), nkb, nxt[0], nxt[1], False, (c,))
                softmax_pv(qi, kb, cur[0], cur[1], False, (c,))
            qi, kb = nqi, nkb
        return qi, kb

    lax.fori_loop(0, n_full // FULL_UNROLL, full_body, (jnp.int32(1), jnp.int32(0)))

    scores(0, 0, sa_ref, bma_ref, True)

    def diag_body(u, carry):
        for step in range(DIAG_UNROLL):
            cur, nxt = bufs[step % 2], bufs[(step + 1) % 2]
            qi = DIAG_UNROLL * u + step
            nqi = jnp.minimum(qi + 1, n_tiles - 1)
            for c in range(n_streams):
                scores(nqi, nqi, nxt[0], nxt[1], True, (c,))
                softmax_pv(qi, qi, cur[0], cur[1], True, (c,))
            finalize(qi)
        return carry

    lax.fori_loop(0, n_tiles // DIAG_UNROLL, diag_body, 0)


def _attention(qt, k0, k1, vt, lq1, lk1, lq2, lk2, g_col, lam_init):
    bn, _, s_len, _ = k0.shape
    tq = ATTN_TILE
    n_tiles = s_len // tq
    n_streams = 2 * tq // SCORE_COLS
    vec = pl.BlockSpec((1, QK_DIM), lambda b, h: (0, 0))
    seq = pl.BlockSpec((None, None, s_len, HEAD_DIM), lambda b, h: (b, h, 0, 0))
    return pl.pallas_call(
        functools.partial(_attn_kernel, lam_init=lam_init),
        grid=(bn, HEADS),
        in_specs=[
            vec, vec, vec, vec,
            pl.BlockSpec((HEAD_DIM, 1), lambda b, h: (0, 0)),
            pl.BlockSpec((None, None, n_tiles, HEAD_DIM, tq), lambda b, h: (b, h, 0, 0, 0)),
            seq, seq,
            pl.BlockSpec((None, None, n_tiles, V_ROWS, tq), lambda b, h: (b, h, 0, 0, 0)),
        ],
        out_specs=pl.BlockSpec((None, s_len, HEAD_DIM), lambda b, h: (b, 0, h)),
        out_shape=jax.ShapeDtypeStruct((bn, s_len, HEADS * HEAD_DIM), BF16),
        scratch_shapes=[
            pltpu.VMEM((n_streams, tq, SCORE_COLS), F32),
            pltpu.VMEM((n_streams, tq, SCORE_COLS), F32),
            pltpu.VMEM((n_streams, 1, SCORE_COLS), F32),
            pltpu.VMEM((n_streams, 1, SCORE_COLS), F32),
            pltpu.VMEM((n_tiles, n_streams, 1, SCORE_COLS), F32),
            pltpu.VMEM((n_tiles, n_streams, V_ROWS, SCORE_COLS), F32),
        ],
        compiler_params=pltpu.CompilerParams(
            dimension_semantics=("arbitrary", "arbitrary"),
            vmem_limit_bytes=ATTN_VMEM_LIMIT_BYTES),
        name="diff_attn",
    )(lq1, lk1, lq2, lk2, g_col, qt, k0, k1, vt)


def _final_kernel(x_ref, p_ref, o_ref, szb_ref, sgb_ref, pa_ref, wbb_ref, wout_ref, wpg_ref,
                  wp_ref, lng_ref, lnb_ref, out_ref, *, alpha):
    for r0 in range(0, x_ref.shape[0], SUB_TILE):
        sub = slice(r0, r0 + SUB_TILE)
        x = x_ref[sub, :]
        xb = x.astype(BF16)
        ple = (_sigmoid(_dot(xb, wpg_ref[...]))
               * _dot(p_ref[sub, :].astype(BF16), wp_ref[...]))
        yb = (o_ref[sub, :].astype(F32) * szb_ref[sub, :].astype(F32)).astype(BF16)
        merged = (pa_ref[sub, :].astype(F32)
                  + sgb_ref[sub, :].astype(F32) * _dot(yb, wbb_ref[...]))
        mix = _dot(merged.astype(BF16), wout_ref[...])
        y = alpha * x + mix + ple
        out_ref[sub, :] = _layer_norm(y, lng_ref[...], lnb_ref[...]).astype(out_ref.dtype)


def _final(x2, p2, o2, szb2, sgb2, pa2, wbb, wout, wpg, wp, lng, lnb, alpha):
    n_tok = x2.shape[0]
    tm = TOKEN_TILE
    tile = lambda w: pl.BlockSpec((tm, w), lambda i: (i, 0))
    full = _resident
    return pl.pallas_call(
        functools.partial(_final_kernel, alpha=alpha),
        grid=(n_tok // tm,),
        in_specs=[
            tile(D_MODEL), tile(PLE_DIM), tile(D_MODEL), tile(D_MODEL), tile(D_MODEL), tile(D_MODEL),
            full((D_MODEL, D_MODEL)), full((D_MODEL, D_MODEL)), full((D_MODEL, D_MODEL)),
            full((PLE_DIM, D_MODEL)), full((1, D_MODEL)), full((1, D_MODEL)),
        ],
        out_specs=tile(D_MODEL),
        out_shape=jax.ShapeDtypeStruct((n_tok, D_MODEL), x2.dtype),
        compiler_params=pltpu.CompilerParams(
            dimension_semantics=("arbitrary",), vmem_limit_bytes=VMEM_LIMIT_BYTES),
        name="merge_out",
    )(x2, p2, o2, szb2, sgb2, pa2, wbb, wout, wpg, wp, lng, lnb)


def _rope_rows():
    inv_freq = ROPE_THETA ** (-jnp.arange(0, QK_DIM, 2, dtype=F32) / QK_DIM)
    ones = jnp.ones_like(inv_freq)
    return (jnp.tile(inv_freq, 4).reshape(1, HEAD_DIM),
            jnp.concatenate([-ones, ones, -ones, ones]).reshape(1, HEAD_DIM))


def kernel(x, p, positions, w_in, a_ln_g, a_ln_b, a_w_s, a_b_s, b_lam_q1, b_lam_k1, b_lam_q2,
           b_lam_k2, b_subln_g, w_branch_a, w_branch_b, w_out, w_ple, w_ple_gate, ln_g, ln_b):
    bn, s_len, d = x.shape
    depth = w_in.shape[0]
    n_tok = bn * s_len
    alpha = (2 * depth) ** 0.25
    freq, sign = _rope_rows()
    pos3 = positions.reshape(bn, s_len, 1)
    row = lambda v: v.reshape(1, -1).astype(F32)
    for i in range(depth):
        lam_init = 0.8 - 0.6 * math.exp(-0.3 * i)
        w_all = w_in[i].astype(BF16)
        x2 = x.reshape(n_tok, d)
        pa = _mixer_a(x2, w_all, row(a_ln_g[i]), row(a_ln_b[i]), a_w_s[i], a_b_s[i].T,
                      w_branch_a[i].astype(BF16))
        qt, k0, k1, vt, szb, sgb = _qkv(x, pos3, freq, sign, w_all)
        o = _attention(qt, k0, k1, vt, row(b_lam_q1[i]), row(b_lam_k1[i]), row(b_lam_q2[i]),
                       row(b_lam_k2[i]), b_subln_g[i].reshape(HEAD_DIM, 1).astype(F32), lam_init)
        y = _final(x2, p[i].reshape(n_tok, PLE_DIM), o.reshape(n_tok, d), szb.reshape(n_tok, d),
                   sgb.reshape(n_tok, d), pa, w_branch_b[i].astype(BF16), w_out[i].astype(BF16),
                   w_ple_gate[i].astype(BF16), w_ple[i].astype(BF16), row(ln_g[i]), row(ln_b[i]),
                   alpha)
        x = y.reshape(bn, s_len, d)
    return x
```

```python
import functools
import math

import jax
import jax.numpy as jnp
from jax import lax
from jax.experimental import pallas as pl
from jax.experimental.pallas import tpu as pltpu

F32 = jnp.float32
BF16 = jnp.bfloat16

D_MODEL = 1024
PLE_DIM = 256
CHUNK = 128
A_GROUPS = 8
HEADS = 8
QK_DIM = 64
HEAD_DIM = 2 * QK_DIM
ROPE_THETA = 10000.0
LN_EPS = 1e-5
RMS_EPS = 1e-5
NEG_BIG = -1e30
(IN_COL_UA, IN_COL_VA, IN_COL_ZA, IN_COL_Q, IN_COL_K, IN_COL_V, IN_COL_ZB, IN_COL_GA,
 IN_COL_GB) = range(9)

TOKEN_TILE = 1024
SUB_TILE = 256
COL_TILE = 256
ATTN_TILE = 512
SCORE_COLS = 256
V_ROWS = HEAD_DIM + 16
FULL_UNROLL = 24
DIAG_UNROLL = 8
VMEM_LIMIT_BYTES = 48 * 1024 * 1024
ATTN_VMEM_LIMIT_BYTES = 56 * 1024 * 1024


def _dot(a, b):
    return jnp.dot(a, b, preferred_element_type=F32)


def _resident(shape):
    return pl.BlockSpec(shape, lambda *_: (0,) * len(shape), pipeline_mode=pl.Buffered(1))


def _resident_cols(j):
    return pl.BlockSpec((D_MODEL, D_MODEL), lambda *_: (0, j), pipeline_mode=pl.Buffered(1))


def _aligned(index, multiple):
    return index if isinstance(index, int) else pl.multiple_of(index, multiple)


def _gelu(x):
    return 0.5 * x * (1.0 + lax.erf(x * math.sqrt(0.5)))


def _sigmoid(x):
    return jax.nn.sigmoid(x)


def _layer_norm(x, g, b):
    mu = jnp.mean(x, axis=-1, keepdims=True)
    xc = x - mu
    var = jnp.mean(xc * xc, axis=-1, keepdims=True)
    return xc * lax.rsqrt(var + LN_EPS) * g + b


def _mixer_a_kernel(x_ref, wu_ref, wv_ref, wz_ref, wg_ref, lng_ref, lnb_ref, ws_ref, bst_ref,
                    wba_ref, out_ref, vb_ref, ya_ref):
    tm = x_ref.shape[0]
    row = lax.broadcasted_iota(jnp.int32, (CHUNK, CHUNK), 0)
    col = lax.broadcasted_iota(jnp.int32, (CHUNK, CHUNK), 1)
    causal = col <= row
    groups_per_block = COL_TILE // CHUNK
    for r0 in range(0, tm, SUB_TILE):
        sub = slice(r0, r0 + SUB_TILE)
        xb = x_ref[sub, :].astype(BF16)
        va = _gelu(_dot(xb, wv_ref[...]))
        ga = _dot(xb, wg_ref[...])
        vb_ref[sub, :] = _layer_norm(va, lng_ref[...], lnb_ref[...]).astype(BF16)
        for cb in range(D_MODEL // COL_TILE):
            cols = slice(cb * COL_TILE, (cb + 1) * COL_TILE)
            u = _gelu(_dot(xb, wu_ref[:, cols]))
            z = _dot(xb, wz_ref[:, cols])
            sz = z * _sigmoid(z)
            for gi in range(groups_per_block):
                g = cb * groups_per_block + gi
                gcols = slice(g * CHUNK, (g + 1) * CHUNK)
                lcols = slice(gi * CHUNK, (gi + 1) * CHUNK)
                ws_g = jnp.where(causal, ws_ref[g], 0.0).astype(BF16)
                bias = bst_ref[:, g:g + 1]
                n_chunks = SUB_TILE // CHUNK
                v_wide = jnp.concatenate(
                    [vb_ref[r0 + n * CHUNK:r0 + (n + 1) * CHUNK, gcols] for n in range(n_chunks)],
                    axis=1)
                sa_wide = _dot(ws_g, v_wide) + bias
                for n in range(n_chunks):
                    lrows = slice(n * CHUNK, (n + 1) * CHUNK)
                    sa = sa_wide[:, lrows]
                    ya_ref[r0 + n * CHUNK:r0 + (n + 1) * CHUNK, gcols] = (
                        u[lrows, lcols] * sa * sz[lrows, lcols]).astype(BF16)
        pa = _dot(ya_ref[sub, :], wba_ref[...])
        out_ref[sub, :] = (_sigmoid(ga) * pa).astype(out_ref.dtype)


def _mixer_a(x2, w_all, lng, lnb, ws, bst, wba):
    n_tok = x2.shape[0]
    tm = TOKEN_TILE
    full = _resident
    return pl.pallas_call(
        _mixer_a_kernel,
        grid=(n_tok // tm,),
        in_specs=[
            pl.BlockSpec((tm, D_MODEL), lambda i: (i, 0)),
            _resident_cols(IN_COL_UA), _resident_cols(IN_COL_VA), _resident_cols(IN_COL_ZA),
            _resident_cols(IN_COL_GA),
            full((1, D_MODEL)), full((1, D_MODEL)),
            full((A_GROUPS, CHUNK, CHUNK)), full((CHUNK, A_GROUPS)),
            full((D_MODEL, D_MODEL)),
        ],
        out_specs=pl.BlockSpec((tm, D_MODEL), lambda i: (i, 0)),
        out_shape=jax.ShapeDtypeStruct((n_tok, D_MODEL), BF16),
        scratch_shapes=[pltpu.VMEM((tm, D_MODEL), BF16), pltpu.VMEM((tm, D_MODEL), BF16)],
        compiler_params=pltpu.CompilerParams(
            dimension_semantics=("arbitrary",), vmem_limit_bytes=VMEM_LIMIT_BYTES),
        name="mixer_a",
    )(x2, w_all, w_all, w_all, w_all, lng, lnb, ws, bst, wba)


def _qkv_kernel(x_ref, pos_ref, freq_ref, sign_ref, wq_ref, wk_ref, wv_ref, wzb_ref, wgb_ref,
                qt_ref, k0_ref, k1_ref, vt_ref, szb_ref, sgb_ref, *, q_scale):
    tm = x_ref.shape[0]
    lane = lax.broadcasted_iota(jnp.int32, (SUB_TILE, HEAD_DIM), 1)
    in_map0 = (lane & QK_DIM) == 0
    in_lo = (lane & (QK_DIM // 2)) == 0
    ta = vt_ref.shape[-1]
    for h in range(HEADS):
        for tile in range(tm // ta):
            vt_ref[h, tile, HEAD_DIM:V_ROWS, :] = jnp.ones((V_ROWS - HEAD_DIM, ta), BF16)

    for r0 in range(0, tm, SUB_TILE):
        sub = slice(r0, r0 + SUB_TILE)
        tile, tcols = r0 // ta, slice(r0 % ta, r0 % ta + SUB_TILE)
        xb = x_ref[sub, :].astype(BF16)

        def gates():
            zb = _dot(xb, wzb_ref[...])
            szb_ref[sub, :] = (zb * _sigmoid(zb)).astype(BF16)
            gb = _dot(xb, wgb_ref[...])
            sgb_ref[sub, :] = _sigmoid(gb).astype(BF16)

        v = _dot(xb, wv_ref[...])
        for h in range(HEADS):
            hs = slice(h * HEAD_DIM, (h + 1) * HEAD_DIM)
            vt_ref[h, tile, 0:HEAD_DIM, tcols] = v[:, hs].astype(BF16).T
        if r0 == 0:
            gates()

        ang = pos_ref[sub, :].astype(F32) * freq_ref[...]
        cos = jnp.cos(ang)
        sin = jnp.sin(ang) * sign_ref[...]

        def rope(t):
            up = pltpu.roll(t, HEAD_DIM - QK_DIM // 2, 1)
            down = pltpu.roll(t, QK_DIM // 2, 1)
            return t * cos + jnp.where(in_lo, up, down) * sin

        q = _dot(xb, wq_ref[...])
        for h in range(HEADS):
            hs = slice(h * HEAD_DIM, (h + 1) * HEAD_DIM)
            qt_ref[h, tile, :, tcols] = (rope(q[:, hs]) * q_scale).astype(BF16).T
        k = _dot(xb, wk_ref[...])
        for h in range(HEADS):
            hs = slice(h * HEAD_DIM, (h + 1) * HEAD_DIM)
            kr = rope(k[:, hs])
            k0_ref[h, sub, :] = jnp.where(in_map0, kr, 0.0).astype(BF16)
            k1_ref[h, sub, :] = jnp.where(in_map0, 0.0, kr).astype(BF16)

        if r0 != 0:
            gates()


def _qkv(x3, pos3, freq, sign, w_all):
    bn, s_len, _ = x3.shape
    tm = TOKEN_TILE
    ta = ATTN_TILE
    per_step = tm // ta
    lane_row = _resident((1, HEAD_DIM))
    return pl.pallas_call(
        functools.partial(_qkv_kernel, q_scale=QK_DIM ** -0.5 * math.log2(math.e)),
        grid=(bn, s_len // tm),
        in_specs=[
            pl.BlockSpec((None, tm, D_MODEL), lambda b, i: (b, i, 0)),
            pl.BlockSpec((None, tm, 1), lambda b, i: (b, i, 0)),
            lane_row, lane_row,
            _resident_cols(IN_COL_Q), _resident_cols(IN_COL_K), _resident_cols(IN_COL_V),
            _resident_cols(IN_COL_ZB), _resident_cols(IN_COL_GB),
        ],
        out_specs=[
            pl.BlockSpec((None, HEADS, per_step, HEAD_DIM, ta), lambda b, i: (b, 0, i, 0, 0)),
            pl.BlockSpec((None, HEADS, tm, HEAD_DIM), lambda b, i: (b, 0, i, 0)),
            pl.BlockSpec((None, HEADS, tm, HEAD_DIM), lambda b, i: (b, 0, i, 0)),
            pl.BlockSpec((None, HEADS, per_step, V_ROWS, ta), lambda b, i: (b, 0, i, 0, 0)),
            pl.BlockSpec((None, tm, D_MODEL), lambda b, i: (b, i, 0)),
            pl.BlockSpec((None, tm, D_MODEL), lambda b, i: (b, i, 0)),
        ],
        out_shape=[
            jax.ShapeDtypeStruct((bn, HEADS, s_len // ta, HEAD_DIM, ta), BF16),
            jax.ShapeDtypeStruct((bn, HEADS, s_len, HEAD_DIM), BF16),
            jax.ShapeDtypeStruct((bn, HEADS, s_len, HEAD_DIM), BF16),
            jax.ShapeDtypeStruct((bn, HEADS, s_len // ta, V_ROWS, ta), BF16),
            jax.ShapeDtypeStruct((bn, s_len, D_MODEL), BF16),
            jax.ShapeDtypeStruct((bn, s_len, D_MODEL), BF16),
        ],
        compiler_params=pltpu.CompilerParams(
            dimension_semantics=("arbitrary", "arbitrary"), vmem_limit_bytes=VMEM_LIMIT_BYTES),
        name="qkv_proj",
    )(x3, pos3, freq, sign, w_all, w_all, w_all, w_all, w_all)


def _attn_kernel(lq1_ref, lk1_ref, lq2_ref, lk2_ref, g_ref, qt_ref, k0_ref, k1_ref, vt_ref, o_ref,
                 sa_ref, sb_ref, bma_ref, bmb_ref, m_ref, acc_ref, *, lam_init):
    tq = ATTN_TILE
    tk = ATTN_TILE
    n_tiles = qt_ref.shape[0]
    n_streams = 2 * tq // SCORE_COLS
    per_map = n_streams // 2

    m_ref[...] = jnp.full(m_ref.shape, NEG_BIG, F32)
    acc_ref[...] = jnp.zeros_like(acc_ref)
    lam = (jnp.exp(jnp.sum(lq1_ref[...] * lk1_ref[...], keepdims=True))
           - jnp.exp(jnp.sum(lq2_ref[...] * lk2_ref[...], keepdims=True)) + lam_init)

    def scores(qi, kb, s_ref, bm_ref, on_diagonal, streams=range(n_streams)):
        k_start = _aligned(kb * tk, tk)
        for c in streams:
            qoff = (c % per_map) * SCORE_COLS
            rows = min(tk, qoff + SCORE_COLS) if on_diagonal else tk
            k_ref = k0_ref if c < per_map else k1_ref
            s = _dot(k_ref[pl.ds(k_start, rows), :], qt_ref[qi, :, qoff:qoff + SCORE_COLS])
            s_ref[c, 0:rows, :] = s
            if not on_diagonal:
                bm_ref[c] = jnp.max(s, axis=0, keepdims=True)

    def softmax_pv(qi, kb, s_ref, bm_ref, on_diagonal, streams=range(n_streams)):
        for c in streams:
            qoff = (c % per_map) * SCORE_COLS
            rows = min(tk, qoff + SCORE_COLS) if on_diagonal else tk
            st = s_ref[c, 0:rows, :]
            if on_diagonal:
                krow = lax.broadcasted_iota(jnp.int32, st.shape, 0)
                qcol = lax.broadcasted_iota(jnp.int32, st.shape, 1) + qoff
                st = jnp.where(krow <= qcol, st, NEG_BIG)
                bm = jnp.max(st, axis=0, keepdims=True)
            else:
                bm = bm_ref[c]
            m_old = m_ref[qi, c]
            m_new = jnp.maximum(m_old, bm)
            alpha = jnp.exp2(m_old - m_new)
            pt = jnp.exp2(st - m_new)
            m_ref[qi, c] = m_new
            acc_ref[qi, c] = alpha * acc_ref[qi, c] + _dot(vt_ref[kb, :, 0:rows], pt.astype(BF16))

    def finalize(qi):
        for j in range(per_map):
            a0 = acc_ref[qi, j]
            a1 = acc_ref[qi, per_map + j]
            ot = (a0[0:HEAD_DIM] / a0[HEAD_DIM:HEAD_DIM + 1]
                  - lam * (a1[0:HEAD_DIM] / a1[HEAD_DIM:HEAD_DIM + 1]))
            ms = jnp.mean(ot * ot, axis=0, keepdims=True)
            ot = ot * lax.rsqrt(ms + RMS_EPS) * g_ref[...] * (1.0 - lam_init)
            row0 = _aligned(qi * tq + j * SCORE_COLS, SCORE_COLS)
            o_ref[pl.ds(row0, SCORE_COLS), :] = ot.T.astype(o_ref.dtype)

    bufs = ((sa_ref, bma_ref), (sb_ref, bmb_ref))

    n_full = n_tiles * (n_tiles - 1) // 2
    assert n_full % FULL_UNROLL == 0 and FULL_UNROLL % 2 == 0
    assert n_tiles % DIAG_UNROLL == 0 and DIAG_UNROLL % 2 == 0

    def advance(qi, kb):
        wrap = kb + 1 >= qi
        return jnp.where(wrap, qi + 1, qi), jnp.where(wrap, 0, kb + 1)

    scores(1, 0, sa_ref, bma_ref, False)

    def full_body(_, carry):
        qi, kb = carry
        for step in range(FULL_UNROLL):
            cur, nxt = bufs[step % 2], bufs[(step + 1) % 2]
            nqi, nkb = advance(qi, kb)
            tqi = jnp.where(nqi >= n_tiles, 0, nqi)
            for c in range(n_streams):
                scores(tqi, nkb, nxt[0], nxt[1], False, (c,))
                softmax_pv(qi, kb, cur[0], cur[1], False, (c,))
            qi, kb = nqi, nkb
        return qi, kb

    lax.fori_loop(0, n_full // FULL_UNROLL, full_body, (jnp.int32(1), jnp.int32(0)))


    def diag_body(u, carry):
        for step in range(DIAG_UNROLL):
            cur, nxt = bufs[step % 2], bufs[(step + 1) % 2]
            qi = DIAG_UNROLL * u + step
            nqi = jnp.minimum(qi + 1, n_tiles - 1)
            for c in range(n_streams):
                scores(nqi, nqi, nxt[0], nxt[1], True, (c,))
                softmax_pv(qi, qi, cur[0], cur[1], True, (c,))
            finalize(qi)
        return carry

    lax.fori_loop(0, n_tiles // DIAG_UNROLL, diag_body, 0)


def _attention(qt, k0, k1, vt, lq1, lk1, lq2, lk2, g_col, lam_init):
    bn, _, s_len, _ = k0.shape
    tq = ATTN_TILE
    n_tiles = s_len // tq
    n_streams = 2 * tq // SCORE_COLS
    vec = pl.BlockSpec((1, QK_DIM), lambda b, h: (0, 0))
    seq = pl.BlockSpec((None, None, s_len, HEAD_DIM), lambda b, h: (b, h, 0, 0))
    return pl.pallas_call(
        functools.partial(_attn_kernel, lam_init=lam_init),
        grid=(bn, HEADS),
        in_specs=[
            vec, vec, vec, vec,
            pl.BlockSpec((HEAD_DIM, 1), lambda b, h: (0, 0)),
            pl.BlockSpec((None, None, n_tiles, HEAD_DIM, tq), lambda b, h: (b, h, 0, 0, 0)),
            seq, seq,
            pl.BlockSpec((None, None, n_tiles, V_ROWS, tq), lambda b, h: (b, h, 0, 0, 0)),
        ],
        out_specs=pl.BlockSpec((None, s_len, HEAD_DIM), lambda b, h: (b, 0, h)),
        out_shape=jax.ShapeDtypeStruct((bn, s_len, HEADS * HEAD_DIM), BF16),
        scratch_shapes=[
            pltpu.VMEM((n_streams, tq, SCORE_COLS), F32),
            pltpu.VMEM((n_streams, tq, SCORE_COLS), F32),
            pltpu.VMEM((n_streams, 1, SCORE_COLS), F32),
            pltpu.VMEM((n_streams, 1, SCORE_COLS), F32),
            pltpu.VMEM((n_tiles, n_streams, 1, SCORE_COLS), F32),
            pltpu.VMEM((n_tiles, n_streams, V_ROWS, SCORE_COLS), F32),
        ],
        compiler_params=pltpu.CompilerParams(
            dimension_semantics=("arbitrary", "arbitrary"),
            vmem_limit_bytes=ATTN_VMEM_LIMIT_BYTES),
        name="diff_attn",
    )(lq1, lk1, lq2, lk2, g_col, qt, k0, k1, vt)


def _final_kernel(x_ref, p_ref, o_ref, szb_ref, sgb_ref, pa_ref, wbb_ref, wout_ref, wpg_ref,
                  wp_ref, lng_ref, lnb_ref, out_ref, *, alpha):
    for r0 in range(0, x_ref.shape[0], 2 * SUB_TILE):
        subs = [slice(r, r + SUB_TILE) for r in (r0, r0 + SUB_TILE)]
        xs = [x_ref[sub, :] for sub in subs]
        xbs = [x.astype(BF16) for x in xs]
        ples = [_sigmoid(_dot(xb, wpg_ref[...])) * _dot(p_ref[sub, :].astype(BF16), wp_ref[...])
                for xb, sub in zip(xbs, subs)]
        ybs = [(o_ref[sub, :].astype(F32) * szb_ref[sub, :].astype(F32)).astype(BF16)
               for sub in subs]
        mergeds = [pa_ref[sub, :].astype(F32) + sgb_ref[sub, :].astype(F32) * _dot(yb, wbb_ref[...])
                   for yb, sub in zip(ybs, subs)]
        mixes = [_dot(merged.astype(BF16), wout_ref[...]) for merged in mergeds]
        for sub, x, mix, ple in zip(subs, xs, mixes, ples):
            y = alpha * x + mix + ple
            out_ref[sub, :] = _layer_norm(y, lng_ref[...], lnb_ref[...]).astype(out_ref.dtype)


def _final(x2, p2, o2, szb2, sgb2, pa2, wbb, wout, wpg, wp, lng, lnb, alpha):
    n_tok = x2.shape[0]
    tm = TOKEN_TILE
    tile = lambda w: pl.BlockSpec((tm, w), lambda i: (i, 0))
    full = _resident
    return pl.pallas_call(
        functools.partial(_final_kernel, alpha=alpha),
        grid=(n_tok // tm,),
        in_specs=[
            tile(D_MODEL), tile(PLE_DIM), tile(D_MODEL), tile(D_MODEL), tile(D_MODEL), tile(D_MODEL),
            full((D_MODEL, D_MODEL)), full((D_MODEL, D_MODEL)), full((D_MODEL, D_MODEL)),
            full((PLE_DIM, D_MODEL)), full((1, D_MODEL)), full((1, D_MODEL)),
        ],
        out_specs=tile(D_MODEL),
        out_shape=jax.ShapeDtypeStruct((n_tok, D_MODEL), x2.dtype),
        compiler_params=pltpu.CompilerParams(
            dimension_semantics=("arbitrary",), vmem_limit_bytes=VMEM_LIMIT_BYTES),
        name="merge_out",
    )(x2, p2, o2, szb2, sgb2, pa2, wbb, wout, wpg, wp, lng, lnb)


def _rope_rows():
    inv_freq = ROPE_THETA ** (-jnp.arange(0, QK_DIM, 2, dtype=F32) / QK_DIM)
    ones = jnp.ones_like(inv_freq)
    return (jnp.tile(inv_freq, 4).reshape(1, HEAD_DIM),
            jnp.concatenate([-ones, ones, -ones, ones]).reshape(1, HEAD_DIM))


def kernel(x, p, positions, w_in, a_ln_g, a_ln_b, a_w_s, a_b_s, b_lam_q1, b_lam_k1, b_lam_q2,
           b_lam_k2, b_subln_g, w_branch_a, w_branch_b, w_out, w_ple, w_ple_gate, ln_g, ln_b):
    bn, s_len, d = x.shape
    depth = w_in.shape[0]
    n_tok = bn * s_len
    alpha = (2 * depth) ** 0.25
    freq, sign = _rope_rows()
    pos3 = positions.reshape(bn, s_len, 1)
    row = lambda v: v.reshape(1, -1).astype(F32)
    for i in range(depth):
        lam_init = 0.8 - 0.6 * math.exp(-0.3 * i)
        w_all = w_in[i].astype(BF16)
        x2 = x.reshape(n_tok, d)
        pa = _mixer_a(x2, w_all, row(a_ln_g[i]), row(a_ln_b[i]), a_w_s[i], a_b_s[i].T,
                      w_branch_a[i].astype(BF16))
        qt, k0, k1, vt, szb, sgb = _qkv(x, pos3, freq, sign, w_all)
        o = _attention(qt, k0, k1, vt, row(b_lam_q1[i]), row(b_lam_k1[i]), row(b_lam_q2[i]),
                       row(b_lam_k2[i]), b_subln_g[i].reshape(HEAD_DIM, 1).astype(F32), lam_init)
        y = _final(x2, p[i].reshape(n_tok, PLE_DIM), o.reshape(n_tok, d), szb.reshape(n_tok, d),
                   sgb.reshape(n_tok, d), pa, w_branch_b[i].astype(BF16), w_out[i].astype(BF16),
                   w_ple_gate[i].astype(BF16), w_ple[i].astype(BF16), row(ln_g[i]), row(ln_b[i]),
                   alpha)
        x = y.reshape(bn, s_len, d)
    return x
```
